```python
import jax, jax.numpy as jnp
from jax import lax
import numpy as np

D_MODEL = 2048
BATCH = 4
SEQ = 4096
DEPTH = 4

HEAD_DIM = 128
N_HGRN_HEADS = D_MODEL // (2 * HEAD_DIM)
N_ATTN_HEADS = D_MODEL // (2 * HEAD_DIM)
HGRN_WIDTH = N_HGRN_HEADS * HEAD_DIM
ATTN_WIDTH = N_ATTN_HEADS * HEAD_DIM
MIX_WIDTH = HGRN_WIDTH + ATTN_WIDTH
IN_SPLITS = (HGRN_WIDTH,) * 4 + (ATTN_WIDTH,) * 3
IN_PROJ_WIDTH = sum(IN_SPLITS)
HGRN_CHUNK = 64
DILATED_PATTERNS = ((128, 1), (512, 4), (2048, 16))
ROPE_THETA = 10000.0
D_FF_DENSE = 5632
N_EXPERTS = 8
TOP_K = 2
D_FF_EXPERT = 7168
MOE_BLOCK = 256
N_DENSE = (DEPTH + 1) // 2
N_MOE = DEPTH // 2
NORM_EPS = 1e-6

kernel_name = 'hybrid_hgrn2_dilated_attn_moe_trunk'


def rms_norm(z, gain):
    zf = z.astype(jnp.float32)
    zf = zf * lax.rsqrt(jnp.mean(zf * zf, axis=-1, keepdims=True) + NORM_EPS)
    return (zf * gain.astype(jnp.float32)).astype(z.dtype)


def hgrn2_mixer(q, f_logit, v, gate, lower_bound, out_gain):
    B, S, _ = q.shape
    H, Dk, C = N_HGRN_HEADS, HEAD_DIM, HGRN_CHUNK
    N = S // C
    lb = lower_bound.astype(jnp.float32)
    f = lb + (1.0 - lb) * jax.nn.sigmoid(f_logit.astype(jnp.float32))
    log_f = jnp.log(f)
    k = 1.0 - f

    def to_chunks(z):
        return z.reshape(B, N, C, H, Dk).transpose(1, 0, 3, 2, 4)

    qc = to_chunks(jax.nn.silu(q.astype(jnp.float32)))
    kc = to_chunks(k)
    vc = to_chunks(v.astype(jnp.float32))
    gc = to_chunks(log_f)
    causal = jnp.tril(jnp.ones((C, C), dtype=bool))

    def chunk_step(state, inp):
        q_c, k_c, v_c, g_c = inp
        b = jnp.cumsum(g_c, axis=2)
        o_inter = jnp.einsum('bhtk,bhkv->bhtv', q_c * jnp.exp(b), state)
        diff = b[:, :, :, None, :] - b[:, :, None, :, :]
        decay = jnp.exp(jnp.where(causal[:, :, None], diff, -jnp.inf))
        scores = jnp.einsum('bhtk,bhsk,bhtsk->bhts', q_c, k_c, decay)
        o_intra = jnp.einsum('bhts,bhsv->bhtv', scores, v_c)
        b_end = b[:, :, -1, :]
        new_state = jnp.exp(b_end)[..., None] * state + jnp.einsum(
            'bhsk,bhsv->bhkv', k_c * jnp.exp(b_end[:, :, None, :] - b), v_c)
        return new_state, o_inter + o_intra

    state0 = jnp.zeros((B, H, Dk, Dk), jnp.float32)
    _, o = lax.scan(chunk_step, state0, (qc, kc, vc, gc))
    o = o.transpose(1, 0, 3, 2, 4).reshape(B, S, H, Dk)
    o = o * lax.rsqrt(jnp.mean(o * o, axis=-1, keepdims=True) + NORM_EPS)
    o = o * out_gain.astype(jnp.float32).reshape(H, Dk)
    o = o.reshape(B, S, HGRN_WIDTH) * jax.nn.silu(gate.astype(jnp.float32))
    return o.astype(q.dtype)


def apply_rope(z):
    S, Dh = z.shape[1], z.shape[3]
    half = Dh // 2
    inv_freq = ROPE_THETA ** (-jnp.arange(half, dtype=jnp.float32) / half)
    ang = jnp.arange(S, dtype=jnp.float32)[:, None] * inv_freq[None, :]
    cos = jnp.cos(ang)[None, :, None, :]
    sin = jnp.sin(ang)[None, :, None, :]
    zf = z.astype(jnp.float32)
    z1, z2 = zf[..., :half], zf[..., half:]
    return jnp.concatenate([z1 * cos - z2 * sin, z2 * cos + z1 * sin], axis=-1)


def dilated_branch(q, k, v, window, dilation):
    B, S, H, Dh = q.shape
    steps = window // dilation
    L = S // dilation
    nb = -(-L // steps)
    Lp = nb * steps

    def to_blocks(z):
        z = z.reshape(B, L, dilation, H, Dh).transpose(0, 2, 3, 1, 4)
        z = jnp.pad(z, ((0, 0), (0, 0), (0, 0), (0, Lp - L), (0, 0)))
        return z.reshape(B, dilation, H, nb, steps, Dh)

    def with_prev(z):
        prev = jnp.pad(z, ((0, 0), (0, 0), (0, 0), (1, 0), (0, 0), (0, 0)))[:, :, :, :-1]
        return jnp.concatenate([prev, z], axis=4)

    qb = to_blocks(q)
    kw = with_prev(to_blocks(k))
    vw = with_prev(to_blocks(v))
    scores = jnp.einsum('brhnqe,brhnke->brhnqk', qb, kw) * (Dh ** -0.5)
    qi = jnp.arange(steps)[:, None]
    kj = jnp.arange(2 * steps)[None, :]
    dist = steps + qi - kj
    key_idx = (jnp.arange(nb)[:, None, None] - 1) * steps + kj[None]
    mask = (dist >= 0) & (dist <= steps) & (key_idx >= 0)
    scores = jnp.where(mask, scores, -jnp.inf)
    m = jnp.max(scores, axis=-1, keepdims=True)
    p = jnp.exp(scores - m)
    den = jnp.sum(p, axis=-1, keepdims=True)
    out = jnp.einsum('brhnqk,brhnke->brhnqe', p, vw) / den
    lse = (m + jnp.log(den))[..., 0]
    out = out.reshape(B, dilation, H, Lp, Dh)[:, :, :, :L].transpose(0, 3, 1, 2, 4).reshape(B, S, H, Dh)
    lse = lse.reshape(B, dilation, H, Lp)[..., :L].transpose(0, 3, 1, 2).reshape(B, S, H)
    return out, lse


def dilated_attention(q, k, v):
    B, S, _ = q.shape
    shp = (B, S, N_ATTN_HEADS, HEAD_DIM)
    qr = apply_rope(q.reshape(shp))
    kr = apply_rope(k.reshape(shp))
    vf = v.reshape(shp).astype(jnp.float32)
    outs, lses = [], []
    for window, dilation in DILATED_PATTERNS:
        o_p, lse_p = dilated_branch(qr, kr, vf, window, dilation)
        outs.append(o_p)
        lses.append(lse_p)
    weights = jax.nn.softmax(jnp.stack(lses), axis=0)
    out = jnp.einsum('pbsh,pbshd->bshd', weights, jnp.stack(outs))
    return out.reshape(B, S, ATTN_WIDTH).astype(q.dtype)


def swiglu(h, w_gate, w_up, w_down):
    return (jax.nn.silu(h @ w_gate) * (h @ w_up)) @ w_down


def moe_swiglu(h, w_router, w_gate, w_up, w_down):
    B, S, D = h.shape
    T = B * S
    tokens = h.reshape(T, D)
    logits = jnp.dot(tokens, w_router, preferred_element_type=jnp.float32)
    top_logits, top_idx = lax.top_k(logits, TOP_K)
    gates = jax.nn.softmax(top_logits, axis=-1)
    n_assign = T * TOP_K
    flat_expert = top_idx.reshape(n_assign)
    order = jnp.argsort(flat_expert)
    sorted_expert = flat_expert[order]
    sorted_token = (order // TOP_K).astype(jnp.int32)
    sorted_gate = gates.reshape(n_assign)[order]
    counts = jnp.bincount(flat_expert, length=N_EXPERTS)
    padded = (counts + MOE_BLOCK - 1) // MOE_BLOCK * MOE_BLOCK
    starts = jnp.cumsum(counts) - counts
    padded_ends = jnp.cumsum(padded)
    padded_starts = padded_ends - padded
    dest = padded_starts[sorted_expert] + (jnp.arange(n_assign) - starts[sorted_expert])
    n_blocks = -(-n_assign // MOE_BLOCK) + N_EXPERTS
    rows = n_blocks * MOE_BLOCK
    row_token = jnp.full((rows,), T, jnp.int32).at[dest].set(sorted_token)
    row_gate = jnp.zeros((rows,), jnp.float32).at[dest].set(sorted_gate)
    block_expert = jnp.minimum(
        jnp.searchsorted(padded_ends, jnp.arange(n_blocks) * MOE_BLOCK, side='right'), N_EXPERTS - 1)
    tokens_ext = jnp.concatenate([tokens, jnp.zeros((1, D), tokens.dtype)], axis=0)
    xb = tokens_ext[row_token].reshape(n_blocks, MOE_BLOCK, D)

    def expert_block(args):
        x_blk, e = args
        return swiglu(x_blk, w_gate[e], w_up[e], w_down[e])

    yb = lax.map(expert_block, (xb, block_expert))
    y_rows = yb.reshape(rows, D) * row_gate[:, None].astype(yb.dtype)
    y = jnp.zeros((T + 1, D), yb.dtype).at[row_token].add(y_rows)[:T]
    return y.reshape(B, S, D)


def setup_inputs(seed: int = 0) -> dict:
    key = jax.random.key(seed)
    ks = jax.random.split(key, 16)
    f32 = jnp.float32
    nrm = lambda k, shape, scale: jax.random.normal(k, shape, f32) * scale
    return {
        'x': nrm(ks[0], (BATCH, SEQ, D_MODEL), 1.0),
        'norm_mix': 1.0 + nrm(ks[1], (DEPTH, D_MODEL), 0.02),
        'w_in': nrm(ks[2], (DEPTH, D_MODEL, IN_PROJ_WIDTH), D_MODEL ** -0.5),
        'lb_logits': nrm(ks[3], (DEPTH, HGRN_WIDTH), 0.5),
        'hgrn_norm': 1.0 + nrm(ks[4], (DEPTH, HGRN_WIDTH), 0.02),
        'w_out': nrm(ks[5], (DEPTH, MIX_WIDTH, D_MODEL), MIX_WIDTH ** -0.5),
        'norm_ffn': 1.0 + nrm(ks[6], (DEPTH, D_MODEL), 0.02),
        'dense_w_gate': nrm(ks[7], (N_DENSE, D_MODEL, D_FF_DENSE), D_MODEL ** -0.5),
        'dense_w_up': nrm(ks[8], (N_DENSE, D_MODEL, D_FF_DENSE), D_MODEL ** -0.5),
        'dense_w_down': nrm(ks[9], (N_DENSE, D_FF_DENSE, D_MODEL), D_FF_DENSE ** -0.5),
        'moe_router': nrm(ks[10], (N_MOE, D_MODEL, N_EXPERTS), D_MODEL ** -0.5),
        'moe_w_gate': nrm(ks[11], (N_MOE, N_EXPERTS, D_MODEL, D_FF_EXPERT), D_MODEL ** -0.5),
        'moe_w_up': nrm(ks[12], (N_MOE, N_EXPERTS, D_MODEL, D_FF_EXPERT), D_MODEL ** -0.5),
        'moe_w_down': nrm(ks[13], (N_MOE, N_EXPERTS, D_FF_EXPERT, D_MODEL), D_FF_EXPERT ** -0.5),
        'final_norm': 1.0 + nrm(ks[14], (D_MODEL,), 0.02),
    }


def reference(x, norm_mix, w_in, lb_logits, hgrn_norm, w_out, norm_ffn,
              dense_w_gate, dense_w_up, dense_w_down,
              moe_router, moe_w_gate, moe_w_up, moe_w_down, final_norm):
    lb_sm = jax.nn.softmax(lb_logits.astype(jnp.float32), axis=0)
    lower_bounds = jnp.cumsum(lb_sm, axis=0) - lb_sm[0:1]
    split_points = [int(s) for s in np.cumsum(IN_SPLITS)[:-1]]
    for layer in range(DEPTH):
        h = rms_norm(x, norm_mix[layer])
        proj = jnp.einsum('bsd,de->bse', h, w_in[layer])
        hq, hf, hi, hg, aq, ak, av = jnp.split(proj, split_points, axis=-1)
        o_hgrn = hgrn2_mixer(hq, hf, hi, hg, lower_bounds[layer], hgrn_norm[layer])
        o_attn = dilated_attention(aq, ak, av)
        mixed = jnp.concatenate([o_hgrn, o_attn.astype(o_hgrn.dtype)], axis=-1)
        x = x + jnp.einsum('bse,ed->bsd', mixed, w_out[layer]).astype(x.dtype)
        h2 = rms_norm(x, norm_ffn[layer])
        if layer % 2 == 0:
            i = layer // 2
            y = swiglu(h2, dense_w_gate[i], dense_w_up[i], dense_w_down[i])
        else:
            i = layer // 2
            y = moe_swiglu(h2, moe_router[i], moe_w_gate[i], moe_w_up[i], moe_w_down[i])
        x = x + y.astype(x.dtype)
    return rms_norm(x, final_norm)
```

```python
import functools

import jax
import jax.numpy as jnp
from jax import lax
from jax.experimental import pallas as pl
from jax.experimental.pallas import tpu as pltpu

F32 = jnp.float32
BF16 = jnp.bfloat16

HEAD_DIM = 128
N_HGRN_HEADS = 8
N_ATTN_HEADS = 8
HGRN_WIDTH = N_HGRN_HEADS * HEAD_DIM
ATTN_WIDTH = N_ATTN_HEADS * HEAD_DIM
HGRN_CHUNK = 64
HGRN_SUB = 16
DILATED_PATTERNS = ((128, 1), (512, 4), (2048, 16))
ROPE_THETA = 10000.0
N_EXPERTS = 8
NORM_EPS = 1e-6
VMEM_LIMIT = 56 * 1024 * 1024

_NT = (((1,), (1,)), ((), ()))
_TN = (((0,), (0,)), ((), ()))


def _params(semantics):
    return pltpu.CompilerParams(dimension_semantics=semantics, vmem_limit_bytes=VMEM_LIMIT)


def _silu(z):
    return z * jax.nn.sigmoid(z)


def _rms(z, gain):
    ms = jnp.mean(z * z, axis=-1, keepdims=True)
    return z * lax.rsqrt(ms + NORM_EPS) * gain


def _in_proj_kernel(x_ref, g_ref, w_ref, o_ref, h_ref):
    @pl.when(pl.program_id(1) == 0)
    def _():
        h_ref[...] = _rms(x_ref[...], g_ref[...]).astype(BF16)

    o_ref[...] = jnp.dot(h_ref[...], w_ref[...], preferred_element_type=F32).astype(o_ref.dtype)


def norm_matmul(x, gain, w, *, tm=1024, tn=1024):
    T, D = x.shape
    N = w.shape[1]
    tm, tn = min(tm, T), min(tn, N)
    return pl.pallas_call(
        _in_proj_kernel,
        grid=(T // tm, N // tn),
        in_specs=[pl.BlockSpec((tm, D), lambda i, j: (i, 0)),
                  pl.BlockSpec((1, D), lambda i, j: (0, 0)),
                  pl.BlockSpec((D, tn), lambda i, j: (0, j))],
        out_specs=pl.BlockSpec((tm, tn), lambda i, j: (i, j)),
        out_shape=jax.ShapeDtypeStruct((T, N), BF16),
        scratch_shapes=[pltpu.VMEM((tm, D), BF16)],
        compiler_params=_params(("parallel", "arbitrary")),
        name="in_proj",
    )(x, gain.reshape(1, D).astype(F32), w)


def _hgrn_kernel(q_ref, f_ref, i_ref, g_ref, lb_ref, gain_ref, o_ref,
                 st_ref, b_scr, k_scr, v_scr, *, n_chunks):
    C, SUB = HGRN_CHUNK, HGRN_SUB

    @pl.when(pl.program_id(2) == 0)
    def _():
        st_ref[...] = jnp.zeros_like(st_ref)

    lb = lb_ref[...]
    gain = gain_ref[...]
    row = lax.broadcasted_iota(jnp.int32, (C, C), 0)
    col = lax.broadcasted_iota(jnp.int32, (C, C), 1)
    tril = (col <= row).astype(BF16)
    ones = jnp.ones((HEAD_DIM, HEAD_DIM), BF16)
    sub_row = lax.broadcasted_iota(jnp.int32, (SUB, HEAD_DIM), 0)

    def chunk(ci, carry):
        rows = pl.ds(pl.multiple_of(ci * C, C), C)
        f = lb + (1.0 - lb) * jax.nn.sigmoid(f_ref[rows, :].astype(F32))
        g = jnp.log(f)
        kk = 1.0 - f
        q = _silu(q_ref[rows, :].astype(F32))
        v = i_ref[rows, :].astype(F32)
        g_hi = g.astype(BF16)
        g_lo = (g - g_hi.astype(F32)).astype(BF16)
        b = (jnp.dot(tril, g_hi, preferred_element_type=F32)
             + jnp.dot(tril, g_lo, preferred_element_type=F32))
        b_scr[...] = b
        k_scr[...] = kk
        v_scr[...] = v
        v16 = v.astype(BF16)
        b_end = b[C - 1:C, :]
        st = st_ref[...]

        o = lax.dot_general((q * jnp.exp(b)).astype(BF16), st.astype(BF16), _NT,
                            preferred_element_type=F32)

        def off_diag(t0, t1, s0, s1):
            b_mid = b[s1 - 1:s1, :]
            qa = (q[t0:t1] * jnp.exp(b[t0:t1] - b_mid)).astype(BF16)
            ka = (kk[s0:s1] * jnp.exp(b_mid - b[s0:s1])).astype(BF16)
            a = lax.dot_general(qa, ka, _NT, preferred_element_type=F32)
            return jnp.dot(a.astype(BF16), v16[s0:s1], preferred_element_type=F32)

        parts = [None] * (C // SUB)
        span = C
        while span > SUB:
            half = span // 2
            for base in range(0, C, span):
                blk = off_diag(base + half, base + span, base, base + half)
                for j in range(half // SUB):
                    idx = (base + half) // SUB + j
                    piece = blk[j * SUB:(j + 1) * SUB]
                    parts[idx] = piece if parts[idx] is None else parts[idx] + piece
            span = half

        for sb in range(C // SUB):
            r = sb * SUB
            qs = q[r:r + SUB]
            bs = b[r:r + SUB]
            ps = []
            for s in range(SUB):
                b_row = jnp.broadcast_to(b_scr[r + s:r + s + 1, :], (SUB, HEAD_DIM))
                k_row = jnp.broadcast_to(k_scr[r + s:r + s + 1, :], (SUB, HEAD_DIM))
                e = jnp.exp(jnp.where(sub_row >= s, bs - b_row, -jnp.inf))
                ps.append((qs * k_row * e).astype(BF16))
            a = jnp.dot(jnp.concatenate(ps, axis=0), ones, preferred_element_type=F32)
            od = None
            for s in range(SUB):
                v_row = jnp.broadcast_to(v_scr[r + s:r + s + 1, :], (SUB, HEAD_DIM))
                term = a[s * SUB:(s + 1) * SUB] * v_row
                od = term if od is None else od + term
            parts[sb] = od if parts[sb] is None else parts[sb] + od

        o = o + jnp.concatenate(parts, axis=0)
        o = _rms(o, gain) * _silu(g_ref[rows, :].astype(F32))
        o_ref[rows, :] = o.astype(o_ref.dtype)

        kd = (kk * jnp.exp(b_end - b)).astype(BF16)
        st_ref[...] = st * jnp.exp(b_end) + lax.dot_general(v16, kd, _TN,
                                                            preferred_element_type=F32)
        return carry

    lax.fori_loop(0, n_chunks, chunk, 0)


def hgrn2(proj, lower_bound, out_gain, *, batch, seq, tt=512):
    T = proj.shape[0]
    H = N_HGRN_HEADS
    tt = min(tt, seq)
    nt = seq // tt

    def sec(k):
        return pl.BlockSpec((tt, HEAD_DIM), lambda b, h, c, k=k: (b * nt + c, k * H + h))

    vec = pl.BlockSpec((1, HEAD_DIM), lambda b, h, c: (0, h))
    return pl.pallas_call(
        functools.partial(_hgrn_kernel, n_chunks=tt // HGRN_CHUNK),
        grid=(batch, H, nt),
        in_specs=[sec(0), sec(1), sec(2), sec(3), vec, vec],
        out_specs=pl.BlockSpec((tt, HEAD_DIM), lambda b, h, c: (b * nt + c, h)),
        out_shape=jax.ShapeDtypeStruct((T, HGRN_WIDTH), BF16),
        scratch_shapes=[pltpu.VMEM((HEAD_DIM, HEAD_DIM), F32),
                        pltpu.VMEM((HGRN_CHUNK, HEAD_DIM), F32),
                        pltpu.VMEM((HGRN_CHUNK, HEAD_DIM), F32),
                        pltpu.VMEM((HGRN_CHUNK, HEAD_DIM), F32)],
        compiler_params=_params(("parallel", "parallel", "arbitrary")),
        name="hgrn2",
    )(proj, proj, proj, proj,
      lower_bound.reshape(1, HGRN_WIDTH).astype(F32), out_gain.reshape(1, HGRN_WIDTH).astype(F32))


def _attn_kernel(q_ref, k_ref, v_ref, cos_ref, sin_ref, o_ref,
                 qf, kf, vf, o1, o2, o3, l1, l2, l3, *, seq):
    BLK = 128
    ROWS = 512
    scale = HEAD_DIM ** -0.5

    def rope(i, carry):
        rows = pl.ds(pl.multiple_of(i * ROWS, ROWS), ROWS)
        c = cos_ref[rows, :]
        s = sin_ref[rows, :]
        q = q_ref[rows, :].astype(F32)
        k = k_ref[rows, :].astype(F32)
        qf[rows, :] = (q * c + pltpu.roll(q, HEAD_DIM // 2, 1) * s) * scale
        kf[rows, :] = k * c + pltpu.roll(k, HEAD_DIM // 2, 1) * s
        vf[rows, :] = v_ref[rows, :].astype(F32)
        return carry

    lax.fori_loop(0, seq // ROWS, rope, 0)

    qi = lax.broadcasted_iota(jnp.int32, (BLK, 2 * BLK), 0)
    kj = lax.broadcasted_iota(jnp.int32, (BLK, 2 * BLK), 1)
    band = (kj >= qi) & (kj <= qi + BLK)
    causal = (lax.broadcasted_iota(jnp.int32, (BLK, BLK), 1)
              <= lax.broadcasted_iota(jnp.int32, (BLK, BLK), 0))

    def unit(o_out, l_out, d, q_start, k_start, n_keys, mask):
        def rows(start, n):
            return pl.ds(start, n) if d == 1 else pl.ds(start, n, stride=d)

        q = qf[rows(q_start, BLK), :].astype(BF16)
        k = kf[rows(k_start, n_keys), :].astype(BF16)
        v = vf[rows(k_start, n_keys), :].astype(BF16)
        s = lax.dot_general(q, k, _NT, preferred_element_type=F32)
        s = jnp.where(mask, s, -jnp.inf)
        m = jnp.max(s, axis=-1, keepdims=True)
        p = jnp.exp(s - m)
        den = jnp.sum(p, axis=-1, keepdims=True)
        acc = jnp.dot(p.astype(BF16), v, preferred_element_type=F32)
        o_out[rows(q_start, BLK), :] = acc / den
        l_out[rows(q_start, BLK), :] = jnp.broadcast_to(m + jnp.log(den), (BLK, HEAD_DIM))

    for (window, d), o_out, l_out in zip(DILATED_PATTERNS, (o1, o2, o3), (l1, l2, l3)):
        assert window == BLK * d and seq % (BLK * d) == 0
        nb = seq // (BLK * d)

        def residue(r, carry, d=d, nb=nb, o_out=o_out, l_out=l_out):
            unit(o_out, l_out, d, r, r, BLK, causal)

            def block(n, c2):
                base = pl.multiple_of(n * (BLK * d), BLK * d)
                unit(o_out, l_out, d, base + r, base - BLK * d + r, 2 * BLK, band)
                return c2

            lax.fori_loop(1, nb, block, 0)
            return carry

        if d == 1:
            residue(0, 0)
        else:
            lax.fori_loop(0, d, residue, 0)

    def combine(i, carry):
        rows = pl.ds(pl.multiple_of(i * ROWS, ROWS), ROWS)
        a1, a2, a3 = l1[rows, :], l2[rows, :], l3[rows, :]
        m = jnp.maximum(jnp.maximum(a1, a2), a3)
        e1, e2, e3 = jnp.exp(a1 - m), jnp.exp(a2 - m), jnp.exp(a3 - m)
        out = (e1 * o1[rows, :] + e2 * o2[rows, :] + e3 * o3[rows, :]) / (e1 + e2 + e3)
        o_ref[rows, :] = out.astype(o_ref.dtype)
        return carry

    lax.fori_loop(0, seq // ROWS, combine, 0)


def _rope_tables(seq):
    half = HEAD_DIM // 2
    inv_freq = ROPE_THETA ** (-jnp.arange(half, dtype=F32) / half)
    ang = jnp.arange(seq, dtype=F32)[:, None] * inv_freq[None, :]
    cos, sin = jnp.cos(ang), jnp.sin(ang)
    return jnp.concatenate([cos, cos], axis=-1), jnp.concatenate([-sin, sin], axis=-1)


def dilated_attention(proj, *, batch, seq, col0):
    T = proj.shape[0]
    H = N_ATTN_HEADS
    c0 = col0 // HEAD_DIM
    cos, sin = _rope_tables(seq)

    def sec(k):
        return pl.BlockSpec((seq, HEAD_DIM), lambda b, h, k=k: (b, c0 + k * H + h))

    table = pl.BlockSpec((seq, HEAD_DIM), lambda b, h: (0, 0))
    slab = pltpu.VMEM((seq, HEAD_DIM), F32)
    return pl.pallas_call(
        functools.partial(_attn_kernel, seq=seq),
        grid=(batch, H),
        in_specs=[sec(0), sec(1), sec(2), table, table],
        out_specs=pl.BlockSpec((seq, HEAD_DIM), lambda b, h: (b, h)),
        out_shape=jax.ShapeDtypeStruct((T, ATTN_WIDTH), BF16),
        scratch_shapes=[slab] * 9,
        compiler_params=_params(("parallel", "parallel")),
        name="dilated_attn",
    )(proj, proj, proj, cos, sin)


def _out_proj_kernel(*refs, route):
    if route:
        oh_ref, oa_ref, x_ref, w_ref, g_ref, wr_ref, xo_ref, h_ref, gate_ref, idx_ref = refs
    else:
        oh_ref, oa_ref, x_ref, w_ref, g_ref, xo_ref, h_ref = refs
    hw = oh_ref.shape[1]
    y = (jnp.dot(oh_ref[...], w_ref[:hw, :], preferred_element_type=F32)
         + jnp.dot(oa_ref[...], w_ref[hw:, :], preferred_element_type=F32))
    x = x_ref[...] + y
    xo_ref[...] = x
    h = _rms(x, g_ref[...])
    h_ref[...] = h.astype(h_ref.dtype)
    if route:
        logits = jnp.dot(h, wr_ref[...], preferred_element_type=F32,
                         precision=lax.Precision.HIGHEST)
        lane = lax.broadcasted_iota(jnp.int32, logits.shape, 1)
        lg = jnp.where(lane < N_EXPERTS, logits, -jnp.inf)
        m1 = jnp.max(lg, axis=-1, keepdims=True)
        i1 = jnp.min(jnp.where(lg == m1, lane, HEAD_DIM), axis=-1, keepdims=True)
        lg = jnp.where(lane == i1, -jnp.inf, lg)
        m2 = jnp.max(lg, axis=-1, keepdims=True)
        i2 = jnp.min(jnp.where(lg == m2, lane, HEAD_DIM), axis=-1, keepdims=True)
        e = jnp.exp(m2 - m1)
        gate_ref[...] = jnp.where(lane == 0, 1.0 / (1.0 + e), jnp.where(lane == 1, e / (1.0 + e), 0.0))
        idx_ref[...] = jnp.where(lane == 0, i1, jnp.where(lane == 1, i2, 0))


def out_proj_norm(o_hgrn, o_attn, x, w, gain, w_router=None, *, tm=512):
    T, D = x.shape
    tm = min(tm, T)
    route = w_router is not None
    row = lambda width: pl.BlockSpec((tm, width), lambda i: (i, 0))
    whole = lambda a: pl.BlockSpec(a.shape, lambda i: (0, 0), pipeline_mode=pl.Buffered(1))
    gain2 = gain.reshape(1, D).astype(F32)
    args = [o_hgrn, o_attn, x, w, gain2]
    in_specs = [row(o_hgrn.shape[1]), row(o_attn.shape[1]), row(D), whole(w), whole(gain2)]
    out_specs = [row(D), row(D)]
    out_shape = [jax.ShapeDtypeStruct((T, D), F32), jax.ShapeDtypeStruct((T, D), BF16)]
    if route:
        wr = jnp.zeros((D, HEAD_DIM), F32).at[:, :N_EXPERTS].set(w_router.astype(F32))
        args.append(wr)
        in_specs.append(whole(wr))
        out_specs += [row(HEAD_DIM), row(HEAD_DIM)]
        out_shape += [jax.ShapeDtypeStruct((T, HEAD_DIM), F32),
                      jax.ShapeDtypeStruct((T, HEAD_DIM), jnp.int32)]
    return pl.pallas_call(
        functools.partial(_out_proj_kernel, route=route),
        grid=(T // tm,),
        in_specs=in_specs,
        out_specs=out_specs,
        out_shape=out_shape,
        compiler_params=_params(("parallel",)),
        name="out_proj_route" if route else "out_proj",
    )(*args)


def _swiglu_step(h, wg_ref, wu_ref, wd_ref, acc_ref):
    g = jnp.dot(h, wg_ref[...], preferred_element_type=F32)
    u = jnp.dot(h, wu_ref[...], preferred_element_type=F32)
    y = jnp.dot((_silu(g) * u).astype(BF16), wd_ref[...], preferred_element_type=F32)
    first = pl.program_id(1) == 0

    @pl.when(first)
    def _():
        acc_ref[...] = y

    @pl.when(jnp.logical_not(first))
    def _():
        acc_ref[...] += y


def _dense_ffn_kernel(h_ref, wg_ref, wu_ref, wd_ref, x_ref, o_ref, acc_ref):
    _swiglu_step(h_ref[...], wg_ref, wu_ref, wd_ref, acc_ref)

    @pl.when(pl.program_id(1) == pl.num_programs(1) - 1)
    def _():
        o_ref[...] = x_ref[...] + acc_ref[...]


def dense_ffn(h, x, w_gate, w_up, w_down, *, tm=512, tf=512):
    T, D = x.shape
    F = w_gate.shape[1]
    tm, tf = min(tm, T), min(tf, F)
    return pl.pallas_call(
        _dense_ffn_kernel,
        grid=(T // tm, F // tf),
        in_specs=[pl.BlockSpec((tm, D), lambda i, f: (i, 0)),
                  pl.BlockSpec((D, tf), lambda i, f: (0, f)),
                  pl.BlockSpec((D, tf), lambda i, f: (0, f)),
                  pl.BlockSpec((tf, D), lambda i, f: (f, 0)),
                  pl.BlockSpec((tm, D), lambda i, f: (i, 0))],
        out_specs=pl.BlockSpec((tm, D), lambda i, f: (i, 0)),
        out_shape=jax.ShapeDtypeStruct((T, D), F32),
        scratch_shapes=[pltpu.VMEM((tm, D), F32)],
        compiler_params=_params(("parallel", "arbitrary")),
        name="dense_ffn",
    )(h, w_gate, w_up, w_down, x)


def _moe_ffn_kernel(be_ref, na_ref, h_ref, wg_ref, wu_ref, wd_ref, o_ref, acc_ref):
    active = pl.program_id(0) < na_ref[0]

    @pl.when(active)
    def _():
        _swiglu_step(h_ref[...], wg_ref, wu_ref, wd_ref, acc_ref)

    @pl.when(active & (pl.program_id(1) == pl.num_programs(1) - 1))
    def _():
        o_ref[...] = acc_ref[...].astype(o_ref.dtype)


def moe_ffn(rows, block_expert, n_active, w_gate, w_up, w_down, *, tm, tf=512):
    R, D = rows.shape
    F = w_gate.shape[2]
    tf = min(tf, F)
    nf = F // tf

    def blk(i, na):
        return jnp.minimum(i, na[0] - 1)

    def fidx(i, f, na):
        return jnp.where(i < na[0], f, nf - 1)

    return pl.pallas_call(
        _moe_ffn_kernel,
        grid_spec=pltpu.PrefetchScalarGridSpec(
            num_scalar_prefetch=2,
            grid=(R // tm, nf),
            in_specs=[pl.BlockSpec((tm, D), lambda i, f, be, na: (blk(i, na), 0)),
                      pl.BlockSpec((None, D, tf), lambda i, f, be, na: (be[blk(i, na)], 0, fidx(i, f, na))),
                      pl.BlockSpec((None, D, tf), lambda i, f, be, na: (be[blk(i, na)], 0, fidx(i, f, na))),
                      pl.BlockSpec((None, tf, D), lambda i, f, be, na: (be[blk(i, na)], fidx(i, f, na), 0))],
            out_specs=pl.BlockSpec((tm, D), lambda i, f, be, na: (blk(i, na), 0)),
            scratch_shapes=[pltpu.VMEM((tm, D), F32)]),
        out_shape=jax.ShapeDtypeStruct((R, D), BF16),
        compiler_params=_params(("arbitrary", "arbitrary")),
        name="moe_ffn",
    )(block_expert, n_active, rows, w_gate, w_up, w_down)


def _rms_rows_kernel(x_ref, g_ref, o_ref):
    o_ref[...] = _rms(x_ref[...], g_ref[...])


def rms_rows(x, gain, *, tm=512):
    T, D = x.shape
    tm = min(tm, T)
    return pl.pallas_call(
        _rms_rows_kernel,
        grid=(T // tm,),
        in_specs=[pl.BlockSpec((tm, D), lambda i: (i, 0)), pl.BlockSpec((1, D), lambda i: (0, 0))],
        out_specs=pl.BlockSpec((tm, D), lambda i: (i, 0)),
        out_shape=jax.ShapeDtypeStruct((T, D), F32),
        compiler_params=_params(("parallel",)),
        name="final_norm",
    )(x, gain.reshape(1, D).astype(F32))


def moe_layer(h, x, gates, top_idx, w_gate, w_up, w_down, *, tm=512):
    T, D = x.shape
    E = w_gate.shape[0]
    n_assign = 2 * T
    flat_expert = top_idx.reshape(n_assign)
    onehot = (flat_expert[:, None] == jnp.arange(E, dtype=jnp.int32)[None, :]).astype(jnp.int32)
    csum = jnp.cumsum(onehot, axis=0)
    rank = jnp.take_along_axis(csum, flat_expert[:, None], axis=1)[:, 0] - 1
    counts = csum[-1]
    padded = (counts + tm - 1) // tm * tm
    padded_ends = jnp.cumsum(padded)
    dest = (padded_ends - padded)[flat_expert] + rank
    n_blocks = -(-n_assign // tm) + E
    row_token = jnp.zeros((n_blocks * tm,), jnp.int32).at[dest].set(
        jnp.arange(n_assign, dtype=jnp.int32) // 2)
    block_expert = jnp.minimum(
        jnp.searchsorted(padded_ends, jnp.arange(n_blocks, dtype=jnp.int32) * tm, side='right'),
        E - 1).astype(jnp.int32)
    n_active = (padded_ends[-1:] // tm).astype(jnp.int32)
    rows = jnp.take(h, row_token, axis=0)
    y_rows = moe_ffn(rows, block_expert, n_active, w_gate, w_up, w_down, tm=tm)
    picked = jnp.take(y_rows, dest, axis=0).astype(F32).reshape(T, 2, D)
    return x + jnp.sum(picked * gates[:, :, None], axis=1)


def kernel(x, norm_mix, w_in, lb_logits, hgrn_norm, w_out, norm_ffn, dense_w_gate, dense_w_up,
           dense_w_down, moe_router, moe_w_gate, moe_w_up, moe_w_down, final_norm):
    B, S, D = x.shape
    depth = w_in.shape[0]
    lb_sm = jax.nn.softmax(lb_logits.astype(F32), axis=0)
    lower_bounds = jnp.cumsum(lb_sm, axis=0) - lb_sm[0:1]
    xt = x.reshape(B * S, D)
    for layer in range(depth):
        proj = norm_matmul(xt, norm_mix[layer], w_in[layer].astype(BF16))
        o_hgrn = hgrn2(proj, lower_bounds[layer], hgrn_norm[layer], batch=B, seq=S)
        o_attn = dilated_attention(proj, batch=B, seq=S, col0=4 * HGRN_WIDTH)
        i = layer // 2
        if layer % 2 == 0:
            xt, h = out_proj_norm(o_hgrn, o_attn, xt, w_out[layer].astype(BF16), norm_ffn[layer])
            xt = dense_ffn(h, xt, dense_w_gate[i].astype(BF16), dense_w_up[i].astype(BF16),
                           dense_w_down[i].astype(BF16))
        else:
            xt, h, gates, idx = out_proj_norm(o_hgrn, o_attn, xt, w_out[layer].astype(BF16),
                                              norm_ffn[layer], moe_router[i])
            xt = moe_layer(h, xt, gates[:, :2], idx[:, :2], moe_w_gate[i].astype(BF16),
                           moe_w_up[i].astype(BF16), moe_w_down[i].astype(BF16))
    return rms_rows(xt, final_norm).reshape(B, S, D)
```

```python
import functools

import jax
import jax.numpy as jnp
from jax import lax
from jax.experimental import pallas as pl
from jax.experimental.pallas import tpu as pltpu

F32 = jnp.float32
BF16 = jnp.bfloat16

HEAD_DIM = 128
N_HGRN_HEADS = 8
N_ATTN_HEADS = 8
HGRN_WIDTH = N_HGRN_HEADS * HEAD_DIM
ATTN_WIDTH = N_ATTN_HEADS * HEAD_DIM
HGRN_CHUNK = 64
HGRN_SUB = 8
HGRN_HEADS_PER_STEP = 8
DILATED_PATTERNS = ((128, 1), (512, 4), (2048, 16))
ATTN_PAD = 2048
ATTN_UNITS = 16
ROPE_THETA = 10000.0
N_EXPERTS = 8
NORM_EPS = 1e-6
VMEM_LIMIT = 56 * 1024 * 1024

_NT = (((1,), (1,)), ((), ()))
_TN = (((0,), (0,)), ((), ()))


def _params(semantics):
    return pltpu.CompilerParams(dimension_semantics=semantics, vmem_limit_bytes=VMEM_LIMIT)


def _silu(z):
    return z * jax.nn.sigmoid(z)


def _rms(z, gain):
    ms = jnp.mean(z * z, axis=-1, keepdims=True)
    return z * lax.rsqrt(ms + NORM_EPS) * gain


def _in_proj_kernel(x_ref, g_ref, w_ref, o_ref, h_ref):
    @pl.when(pl.program_id(1) == 0)
    def _():
        h_ref[...] = _rms(x_ref[...], g_ref[...]).astype(BF16)

    o_ref[...] = jnp.dot(h_ref[...], w_ref[...], preferred_element_type=F32).astype(o_ref.dtype)


def norm_matmul(x, gain, w, *, tm=1024, tn=1024):
    T, D = x.shape
    N = w.shape[1]
    tm, tn = min(tm, T), min(tn, N)
    return pl.pallas_call(
        _in_proj_kernel,
        grid=(T // tm, N // tn),
        in_specs=[pl.BlockSpec((tm, D), lambda i, j: (i, 0)),
                  pl.BlockSpec((1, D), lambda i, j: (0, 0)),
                  pl.BlockSpec((D, tn), lambda i, j: (0, j))],
        out_specs=pl.BlockSpec((tm, tn), lambda i, j: (i, j)),
        out_shape=jax.ShapeDtypeStruct((T, N), BF16),
        scratch_shapes=[pltpu.VMEM((tm, D), BF16)],
        compiler_params=_params(("parallel", "arbitrary")),
        name="in_proj",
    )(x, gain.reshape(1, D).astype(F32), w)


def _hgrn_kernel(q_ref, f_ref, i_ref, g_ref, lb_ref, gain_ref, o_ref,
                 st_ref, b_ref, k_ref, v_ref, *, n_chunks, heads):
    C, SUB = HGRN_CHUNK, HGRN_SUB

    @pl.when(pl.program_id(2) == 0)
    def _():
        st_ref[...] = jnp.zeros_like(st_ref)

    row = lax.broadcasted_iota(jnp.int32, (C, C), 0)
    col = lax.broadcasted_iota(jnp.int32, (C, C), 1)
    tril = (col <= row).astype(BF16)
    ones = jnp.ones((HEAD_DIM, HEAD_DIM), BF16)
    sub_row = lax.broadcasted_iota(jnp.int32, (SUB, HEAD_DIM), 0)

    pairs = []
    span = C
    while span > SUB:
        half = span // 2
        pairs += [(base + half, base + span, base, base + half) for base in range(0, C, span)]
        span = half

    def gates(hh, rows):
        lanes = slice(hh * HEAD_DIM, (hh + 1) * HEAD_DIM)
        lb = lb_ref[:, lanes]
        f = lb + (1.0 - lb) * jax.nn.sigmoid(f_ref[rows, lanes].astype(F32))
        g = jnp.log(f)
        g_hi = g.astype(BF16)
        g_lo = (g - g_hi.astype(F32)).astype(BF16)
        b = (jnp.dot(tril, g_hi, preferred_element_type=F32)
             + jnp.dot(tril, g_lo, preferred_element_type=F32))
        return dict(lanes=lanes, b=b, kk=1.0 - f, q=_silu(q_ref[rows, lanes].astype(F32)),
                    v=i_ref[rows, lanes].astype(F32))

    def scores(hh, s):
        b, kk, q, v = s["b"], s["kk"], s["q"], s["v"]
        b_scr, k_scr = b_ref.at[hh], k_ref.at[hh]
        b_scr[...] = b
        k_scr[...] = kk
        v_ref[hh] = v
        s["v16"] = v.astype(BF16)
        b_end = b[C - 1:C, :]
        st = st_ref[hh]
        s["o"] = lax.dot_general((q * jnp.exp(b)).astype(BF16), st.astype(BF16), _NT,
                                 preferred_element_type=F32)
        kd = (kk * jnp.exp(b_end - b)).astype(BF16)
        st_ref[hh] = st * jnp.exp(b_end) + lax.dot_general(s["v16"], kd, _TN,
                                                           preferred_element_type=F32)
        s["a_off"] = []
        for t0, t1, s0, s1 in pairs:
            b_mid = b[s1 - 1:s1, :]
            qa = (q[t0:t1] * jnp.exp(b[t0:t1] - b_mid)).astype(BF16)
            ka = (kk[s0:s1] * jnp.exp(b_mid - b[s0:s1])).astype(BF16)
            s["a_off"].append(lax.dot_general(qa, ka, _NT, preferred_element_type=F32))
        ps = []
        for r in range(0, C, SUB):
            qs = q[r:r + SUB]
            bs = b[r:r + SUB]
            for j in range(SUB):
                b_row = jnp.broadcast_to(b_scr[r + j:r + j + 1, :], (SUB, HEAD_DIM))
                k_row = jnp.broadcast_to(k_scr[r + j:r + j + 1, :], (SUB, HEAD_DIM))
                d = bs - b_row if j == 0 else jnp.where(sub_row >= j, bs - b_row, -jnp.inf)
                ps.append(qs * k_row * jnp.exp(d))
        s["a_diag"] = jnp.dot(jnp.concatenate(ps, axis=0).astype(BF16), ones,
                              preferred_element_type=F32)

    def values(hh, s):
        v_scr = v_ref.at[hh]
        parts = [None] * (C // SUB)

        def add(idx, piece):
            parts[idx] = piece if parts[idx] is None else parts[idx] + piece

        for (t0, t1, s0, s1), a in zip(pairs, s["a_off"]):
            blk = jnp.dot(a.astype(BF16), s["v16"][s0:s1], preferred_element_type=F32)
            for j in range((t1 - t0) // SUB):
                add(t0 // SUB + j, blk[j * SUB:(j + 1) * SUB])
        a = s["a_diag"]
        for r in range(0, C, SUB):
            for j in range(SUB):
                v_row = jnp.broadcast_to(v_scr[r + j:r + j + 1, :], (SUB, HEAD_DIM))
                add(r // SUB, a[(r + j) * SUB:(r + j + 1) * SUB] * v_row)
        s["o"] = s["o"] + jnp.concatenate(parts, axis=0)

    def finish(s, rows):
        lanes = s["lanes"]
        o = _rms(s["o"], gain_ref[:, lanes]) * _silu(g_ref[rows, lanes].astype(F32))
        o_ref[rows, lanes] = o.astype(o_ref.dtype)

    def chunk(ci, carry):
        rows = pl.ds(pl.multiple_of(ci * C, C), C)
        state = [gates(hh, rows) for hh in range(heads)]
        for hh in range(heads):
            scores(hh, state[hh])
        for hh in range(heads):
            values(hh, state[hh])
        for hh in range(heads):
            finish(state[hh], rows)
        return carry

    lax.fori_loop(0, n_chunks, chunk, 0)


def hgrn2(proj, lower_bound, out_gain, *, batch, seq, tt=512, heads=HGRN_HEADS_PER_STEP):
    T = proj.shape[0]
    H = N_HGRN_HEADS // heads
    width = heads * HEAD_DIM
    tt = min(tt, seq)
    nt = seq // tt

    def sec(k):
        return pl.BlockSpec((tt, width), lambda b, h, c, k=k: (b * nt + c, k * H + h))

    vec = pl.BlockSpec((1, width), lambda b, h, c: (0, h))
    per_head = pltpu.VMEM((heads, HGRN_CHUNK, HEAD_DIM), F32)
    return pl.pallas_call(
        functools.partial(_hgrn_kernel, n_chunks=tt // HGRN_CHUNK, heads=heads),
        grid=(batch, H, nt),
        in_specs=[sec(0), sec(1), sec(2), sec(3), vec, vec],
        out_specs=pl.BlockSpec((tt, width), lambda b, h, c: (b * nt + c, h)),
        out_shape=jax.ShapeDtypeStruct((T, HGRN_WIDTH), BF16),
        scratch_shapes=[pltpu.VMEM((heads, HEAD_DIM, HEAD_DIM), F32), per_head, per_head, per_head],
        compiler_params=_params(("parallel", "parallel", "arbitrary")),
        name="hgrn2",
    )(proj, proj, proj, proj,
      lower_bound.reshape(1, HGRN_WIDTH).astype(F32), out_gain.reshape(1, HGRN_WIDTH).astype(F32))


def _attn_kernel(q_ref, k_ref, v_ref, cos_ref, sin_ref, o_ref,
                 qf, kf, vf, o1, o2, o3, l1, l2, l3, bias_ref, *, seq):
    BLK = 128
    ROWS = 512
    scale = HEAD_DIM ** -0.5

    def rope(i, carry):
        rows = pl.ds(pl.multiple_of(i * ROWS, ROWS), ROWS)
        c = cos_ref[rows, :]
        s = sin_ref[rows, :]
        q = q_ref[rows, :].astype(F32)
        k = k_ref[rows, :].astype(F32)
        qf[rows, :] = (q * c + pltpu.roll(q, HEAD_DIM // 2, 1) * s) * scale
        padded = pl.ds(pl.multiple_of(i * ROWS, ROWS) + ATTN_PAD, ROWS)
        kf[padded, :] = k * c + pltpu.roll(k, HEAD_DIM // 2, 1) * s
        vf[padded, :] = v_ref[rows, :].astype(F32)
        return carry

    lax.fori_loop(0, seq // ROWS, rope, 0)

    kf[0:ATTN_PAD, :] = jnp.zeros((ATTN_PAD, HEAD_DIM), F32)
    vf[0:ATTN_PAD, :] = jnp.zeros((ATTN_PAD, HEAD_DIM), F32)
    qi = lax.broadcasted_iota(jnp.int32, (BLK, 2 * BLK), 0)
    kj = lax.broadcasted_iota(jnp.int32, (BLK, 2 * BLK), 1)
    band = (kj >= qi) & (kj <= qi + BLK)
    bias_ref[0] = jnp.where(band & (kj >= BLK), 0.0, -jnp.inf)
    bias_ref[1] = jnp.where(band, 0.0, -jnp.inf)
    ones = jnp.ones((2 * BLK, HEAD_DIM), BF16)

    def rows(d, start, size):
        return pl.ds(start, size) if d == 1 else pl.ds(start, size, stride=d)

    def qk(d, u):
        n = lax.shift_right_logical(u, d.bit_length() - 1)
        q_start = n * (BLK * d) + (u & (d - 1))
        k_start = q_start + (ATTN_PAD - BLK * d)
        q = qf[rows(d, q_start, BLK), :].astype(BF16)
        k = kf[rows(d, k_start, 2 * BLK), :].astype(BF16)
        v = vf[rows(d, k_start, 2 * BLK), :].astype(BF16)
        s = lax.dot_general(q, k, _NT, preferred_element_type=F32)
        return dict(n=n, q_start=q_start, s=s, v=v)

    def pv(t):
        s = t["s"] + bias_ref[jnp.minimum(t["n"], 1)]
        t["m"] = jnp.max(s, axis=-1, keepdims=True)
        p = jnp.exp(s - t["m"]).astype(BF16)
        t["acc"] = jnp.dot(p, jnp.concatenate([t["v"], ones], axis=1), preferred_element_type=F32)

    def store(d, t, o_out, l_out):
        den = t["acc"][:, HEAD_DIM:]
        o_out[rows(d, t["q_start"], BLK), :] = t["acc"][:, :HEAD_DIM] / den
        l_out[rows(d, t["q_start"], BLK), :] = t["m"] + jnp.log(den)

    for (window, d), o_out, l_out in zip(DILATED_PATTERNS, (o1, o2, o3), (l1, l2, l3)):
        assert window == BLK * d and seq % (BLK * d) == 0 and BLK * d <= ATTN_PAD
        n_units = seq // BLK
        assert n_units % ATTN_UNITS == 0

        def units(it, carry, d=d, o_out=o_out, l_out=l_out):
            ts = [qk(d, it * ATTN_UNITS + j) for j in range(ATTN_UNITS)]
            for t in ts:
                pv(t)
            for t in ts:
                store(d, t, o_out, l_out)
            return carry

        lax.fori_loop(0, n_units // ATTN_UNITS, units, 0)

    def combine(i, carry):
        rows = pl.ds(pl.multiple_of(i * ROWS, ROWS), ROWS)
        a1, a2, a3 = l1[rows, :], l2[rows, :], l3[rows, :]
        m = jnp.maximum(jnp.maximum(a1, a2), a3)
        e1, e2, e3 = jnp.exp(a1 - m), jnp.exp(a2 - m), jnp.exp(a3 - m)
        out = (e1 * o1[rows, :] + e2 * o2[rows, :] + e3 * o3[rows, :]) / (e1 + e2 + e3)
        o_ref[rows, :] = out.astype(o_ref.dtype)
        return carry

    lax.fori_loop(0, seq // ROWS, combine, 0)


def _rope_tables(seq):
    half = HEAD_DIM // 2
    inv_freq = ROPE_THETA ** (-jnp.arange(half, dtype=F32) / half)
    ang = jnp.arange(seq, dtype=F32)[:, None] * inv_freq[None, :]
    cos, sin = jnp.cos(ang), jnp.sin(ang)
    return jnp.concatenate([cos, cos], axis=-1), jnp.concatenate([-sin, sin], axis=-1)


def dilated_attention(proj, *, batch, seq, col0):
    T = proj.shape[0]
    H = N_ATTN_HEADS
    c0 = col0 // HEAD_DIM
    cos, sin = _rope_tables(seq)

    def sec(k):
        return pl.BlockSpec((seq, HEAD_DIM), lambda b, h, k=k: (b, c0 + k * H + h))

    table = pl.BlockSpec((seq, HEAD_DIM), lambda b, h: (0, 0))
    slab = pltpu.VMEM((seq, HEAD_DIM), F32)
    padded = pltpu.VMEM((ATTN_PAD + seq, HEAD_DIM), F32)
    return pl.pallas_call(
        functools.partial(_attn_kernel, seq=seq),
        grid=(batch, H),
        in_specs=[sec(0), sec(1), sec(2), table, table],
        out_specs=pl.BlockSpec((seq, HEAD_DIM), lambda b, h: (b, h)),
        out_shape=jax.ShapeDtypeStruct((T, ATTN_WIDTH), BF16),
        scratch_shapes=[slab, padded, padded] + [slab] * 6
                       + [pltpu.VMEM((2, HEAD_DIM, 2 * HEAD_DIM), F32)],
        compiler_params=_params(("parallel", "parallel")),
        name="dilated_attn",
    )(proj, proj, proj, cos, sin)


def _out_proj_kernel(*refs, route):
    if route:
        oh_ref, oa_ref, x_ref, w_ref, g_ref, wr_ref, xo_ref, h_ref, gate_ref, idx_ref = refs
    else:
        oh_ref, oa_ref, x_ref, w_ref, g_ref, xo_ref, h_ref = refs
    hw = oh_ref.shape[1]
    y = (jnp.dot(oh_ref[...], w_ref[:hw, :], preferred_element_type=F32)
         + jnp.dot(oa_ref[...], w_ref[hw:, :], preferred_element_type=F32))
    x = x_ref[...] + y
    xo_ref[...] = x
    h = _rms(x, g_ref[...])
    h_ref[...] = h.astype(h_ref.dtype)
    if route:
        h_hi = h.astype(BF16)
        h_lo = (h - h_hi.astype(F32)).astype(BF16)
        logits = (jnp.dot(h_hi, wr_ref[0], preferred_element_type=F32)
                  + jnp.dot(h_lo, wr_ref[0], preferred_element_type=F32)
                  + jnp.dot(h_hi, wr_ref[1], preferred_element_type=F32))
        lane = lax.broadcasted_iota(jnp.int32, logits.shape, 1)
        lg = jnp.where(lane < N_EXPERTS, logits, -jnp.inf)
        m1 = jnp.max(lg, axis=-1, keepdims=True)
        i1 = jnp.min(jnp.where(lg == m1, lane, HEAD_DIM), axis=-1, keepdims=True)
        lg = jnp.where(lane == i1, -jnp.inf, lg)
        m2 = jnp.max(lg, axis=-1, keepdims=True)
        i2 = jnp.min(jnp.where(lg == m2, lane, HEAD_DIM), axis=-1, keepdims=True)
        e = jnp.exp(m2 - m1)
        gate_ref[...] = jnp.where(lane == 0, 1.0 / (1.0 + e), jnp.where(lane == 1, e / (1.0 + e), 0.0))
        idx_ref[...] = jnp.where(lane == 0, i1, jnp.where(lane == 1, i2, 0))


def out_proj_norm(o_hgrn, o_attn, x, w, gain, w_router=None, *, tm=512):
    T, D = x.shape
    tm = min(tm, T)
    route = w_router is not None
    row = lambda width: pl.BlockSpec((tm, width), lambda i: (i, 0))
    whole = lambda a: pl.BlockSpec(a.shape, lambda i: (0, 0), pipeline_mode=pl.Buffered(1))
    gain2 = gain.reshape(1, D).astype(F32)
    args = [o_hgrn, o_attn, x, w, gain2]
    in_specs = [row(o_hgrn.shape[1]), row(o_attn.shape[1]), row(D), whole(w), whole(gain2)]
    out_specs = [row(D), row(D)]
    out_shape = [jax.ShapeDtypeStruct((T, D), F32), jax.ShapeDtypeStruct((T, D), F32 if route else BF16)]
    if route:
        wr = jnp.zeros((D, HEAD_DIM), F32).at[:, :N_EXPERTS].set(w_router.astype(F32))
        wr_hi = wr.astype(BF16)
        wr = jnp.stack([wr_hi, (wr - wr_hi.astype(F32)).astype(BF16)])
        args.append(wr)
        in_specs.append(pl.BlockSpec(wr.shape, lambda i: (0, 0, 0), pipeline_mode=pl.Buffered(1)))
        out_specs += [row(HEAD_DIM), row(HEAD_DIM)]
        out_shape += [jax.ShapeDtypeStruct((T, HEAD_DIM), F32),
                      jax.ShapeDtypeStruct((T, HEAD_DIM), jnp.int32)]
    return pl.pallas_call(
        functools.partial(_out_proj_kernel, route=route),
        grid=(T // tm,),
        in_specs=in_specs,
        out_specs=out_specs,
        out_shape=out_shape,
        compiler_params=_params(("parallel",)),
        name="out_proj_route" if route else "out_proj",
    )(*args)


def _fresh_weights(be_ref, i):
    return (i == 0) | (be_ref[i] != be_ref[jnp.maximum(i - 1, 0)])


def _ffn_up_kernel(be_ref, na_ref, x_ref, wg_ref, wu_ref, a_ref, wg16, wu16):
    i = pl.program_id(1)
    active = i < na_ref[0]

    @pl.when(active & _fresh_weights(be_ref, i))
    def _():
        wg16[...] = wg_ref[...].astype(BF16)
        wu16[...] = wu_ref[...].astype(BF16)

    @pl.when(active)
    def _():
        x = x_ref[...]
        g = jnp.dot(x, wg16[...], preferred_element_type=F32)
        u = jnp.dot(x, wu16[...], preferred_element_type=F32)
        a_ref[...] = (_silu(g) * u).astype(a_ref.dtype)

    @pl.when(jnp.logical_not(active))
    def _():
        a_ref[...] = jnp.zeros_like(a_ref)


def _ffn_down_kernel(*refs, residual):
    if residual:
        be_ref, na_ref, a_ref, wd_ref, x_ref, o_ref, wd16 = refs
    else:
        be_ref, na_ref, a_ref, wd_ref, o_ref, wd16 = refs
    i = pl.program_id(1)
    active = i < na_ref[0]

    @pl.when(active & _fresh_weights(be_ref, i))
    def _():
        wd16[...] = wd_ref[...].astype(BF16)

    @pl.when(active)
    def _():
        y = jnp.dot(a_ref[...], wd16[...], preferred_element_type=F32)
        o_ref[...] = x_ref[...] + y if residual else y

    @pl.when(jnp.logical_not(active))
    def _():
        o_ref[...] = jnp.zeros_like(o_ref)


def _block(i, na):
    return jnp.minimum(i, na[0] - 1)


def ffn_up(rows, block_expert, n_active, w_gate, w_up, *, tm, tf):
    R, D = rows.shape
    F = w_gate.shape[2]
    tf = min(tf, F)
    wspec =pl.BlockSpec((None, D, tf), lambda f, i, be, na: (be[_block(i, na)], 0, f))
    return pl.pallas_call(
        _ffn_up_kernel,
        grid_spec=pltpu.PrefetchScalarGridSpec(
            num_scalar_prefetch=2,
            grid=(F // tf, R // tm),
            in_specs=[pl.BlockSpec((tm, D), lambda f, i, be, na: (_block(i, na), 0)), wspec, wspec],
            out_specs=pl.BlockSpec((tm, tf), lambda f, i, be, na: (i, f)),
            scratch_shapes=[pltpu.VMEM((D, tf), BF16), pltpu.VMEM((D, tf), BF16)]),
        out_shape=jax.ShapeDtypeStruct((R, F), BF16),
        compiler_params=_params(("arbitrary", "arbitrary")),
        name="ffn_up",
    )(block_expert, n_active, rows, w_gate, w_up)


def ffn_down(act, block_expert, n_active, w_down, x=None, *, tm, tn):
    R, F = act.shape
    D = w_down.shape[2]
    tn = min(tn, D)
    residual = x is not None
    tile = pl.BlockSpec((tm, tn), lambda n, i, be, na: (i, n))
    in_specs = [pl.BlockSpec((tm, F), lambda n, i, be, na: (_block(i, na), 0)),
                pl.BlockSpec((None, F, tn), lambda n, i, be, na: (be[_block(i, na)], 0, n))]
    args = [act, w_down]
    if residual:
        in_specs.append(tile)
        args.append(x)
    return pl.pallas_call(
        functools.partial(_ffn_down_kernel, residual=residual),
        grid_spec=pltpu.PrefetchScalarGridSpec(
            num_scalar_prefetch=2,
            grid=(D // tn, R // tm),
            in_specs=in_specs,
            out_specs=tile,
            scratch_shapes=[pltpu.VMEM((F, tn), BF16)]),
        out_shape=jax.ShapeDtypeStruct((R, D), F32),
        compiler_params=_params(("arbitrary", "arbitrary")),
        name="ffn_down",
    )(block_expert, n_active, *args)


def dense_ffn(h, x, w_gate, w_up, w_down, index, *, tm=1024, tf=512, tm_down=512, tn=512):
    T = x.shape[0]

    def blocks(t):
        return jnp.full((T // t,), index, jnp.int32), jnp.full((1,), T // t, jnp.int32)

    act = ffn_up(h, *blocks(tm), w_gate, w_up, tm=tm, tf=tf)
    return ffn_down(act, *blocks(tm_down), w_down, x, tm=tm_down, tn=tn)


def _row_copy(src_hbm, dst_ref, sem, src_row, dst_row):
    return pltpu.make_async_copy(src_hbm.at[pl.ds(src_row, 1), :], dst_ref.at[pl.ds(dst_row, 1), :], sem)


def _gather_rows_kernel(na_ref, idx_ref, h_hbm, o_ref, buf, sem):
    tm = buf.shape[0]

    @pl.when(pl.program_id(0) < na_ref[0])
    def _():
        def start(r, c):
            _row_copy(h_hbm, buf, sem, idx_ref[0, 0, r], r).start()
            return c

        def wait(r, c):
            _row_copy(h_hbm, buf, sem, idx_ref[0, 0, r], r).wait()
            return c

        lax.fori_loop(0, tm, start, 0)
        lax.fori_loop(0, tm, wait, 0)
        o_ref[...] = buf[...].astype(o_ref.dtype)

    @pl.when(pl.program_id(0) >= na_ref[0])
    def _():
        o_ref[...] = jnp.zeros_like(o_ref)


def gather_rows(h, row_token, n_active, *, tm):
    T, D = h.shape
    R = row_token.shape[0]
    return pl.pallas_call(
        _gather_rows_kernel,
        grid_spec=pltpu.PrefetchScalarGridSpec(
            num_scalar_prefetch=1,
            grid=(R // tm,),
            in_specs=[pl.BlockSpec((1, 1, tm), lambda i, na: (_block(i, na), 0, 0),
                                   memory_space=pltpu.SMEM),
                      pl.BlockSpec(memory_space=pl.ANY)],
            out_specs=pl.BlockSpec((tm, D), lambda i, na: (i, 0)),
            scratch_shapes=[pltpu.VMEM((tm, D), F32), pltpu.SemaphoreType.DMA(())]),
        out_shape=jax.ShapeDtypeStruct((R, D), BF16),
        compiler_params=pltpu.CompilerParams(dimension_semantics=("arbitrary",),
                                             vmem_limit_bytes=VMEM_LIMIT),
        name="gather_rows",
    )(n_active, row_token.reshape(R // tm, 1, tm), h)


def _combine_kernel(*refs, final):
    if final:
        dest_ref, x_ref, gate_ref, gain_ref, y_hbm, o_ref, buf0, buf1, sem = refs
    else:
        dest_ref, x_ref, gate_ref, y_hbm, o_ref, buf0, buf1, sem = refs
    tm = buf0.shape[0]

    def copies(r):
        return (_row_copy(y_hbm, buf0, sem.at[0], dest_ref[0, 0, 2 * r], r),
                _row_copy(y_hbm, buf1, sem.at[1], dest_ref[0, 0, 2 * r + 1], r))

    def start(r, c):
        for cp in copies(r):
            cp.start()
        return c

    def wait(r, c):
        for cp in copies(r):
            cp.wait()
        return c

    lax.fori_loop(0, tm, start, 0)
    lax.fori_loop(0, tm, wait, 0)
    gates = gate_ref[...]
    x = x_ref[...] + gates[:, 0:1] * buf0[...] + gates[:, 1:2] * buf1[...]
    o_ref[...] = _rms(x, gain_ref[...]) if final else x


def combine_rows(x, y_rows, dest, gates, final_gain=None, *, tm=256):
    T, D = x.shape
    final = final_gain is not None
    row = pl.BlockSpec((tm, D), lambda i: (i, 0))
    in_specs = [pl.BlockSpec((1, 1, 2 * tm), lambda i: (i, 0, 0), memory_space=pltpu.SMEM),
                row, pl.BlockSpec((tm, gates.shape[1]), lambda i: (i, 0))]
    args = [dest.reshape(T // tm, 1, 2 * tm), x, gates]
    if final:
        in_specs.append(pl.BlockSpec((1, D), lambda i: (0, 0)))
        args.append(final_gain.reshape(1, D).astype(F32))
    in_specs.append(pl.BlockSpec(memory_space=pl.ANY))
    args.append(y_rows)
    return pl.pallas_call(
        functools.partial(_combine_kernel, final=final),
        grid=(T // tm,),
        in_specs=in_specs,
        out_specs=row,
        out_shape=jax.ShapeDtypeStruct((T, D), F32),
        scratch_shapes=[pltpu.VMEM((tm, D), F32), pltpu.VMEM((tm, D), F32),
                        pltpu.SemaphoreType.DMA((2,))],
        compiler_params=_params(("arbitrary",)),
        name="combine_rows",
    )(*args)


def _rms_rows_kernel(x_ref, g_ref, o_ref):
    o_ref[...] = _rms(x_ref[...], g_ref[...])


def rms_rows(x, gain, *, tm=512):
    T, D = x.shape
    tm = min(tm, T)
    return pl.pallas_call(
        _rms_rows_kernel,
        grid=(T // tm,),
        in_specs=[pl.BlockSpec((tm, D), lambda i: (i, 0)), pl.BlockSpec((1, D), lambda i: (0, 0))],
        out_specs=pl.BlockSpec((tm, D), lambda i: (i, 0)),
        out_shape=jax.ShapeDtypeStruct((T, D), F32),
        compiler_params=_params(("parallel",)),
        name="final_norm",
    )(x, gain.reshape(1, D).astype(F32))


def moe_layer(h, x, gates, top_idx, w_gate, w_up, w_down, index, final_gain=None, *,
              tm=512, tm_down=256):
    T, D = x.shape
    E = w_gate.shape[1]
    w_gate, w_up, w_down = (w.reshape((-1,) + w.shape[2:]) for w in (w_gate, w_up, w_down))
    n_assign = 2 * T
    flat_expert = top_idx[:, :2].reshape(n_assign)
    onehot = (flat_expert[:, None] == jnp.arange(E, dtype=jnp.int32)[None, :]).astype(jnp.int32)
    csum = jnp.cumsum(onehot, axis=0)
    rank = jnp.take_along_axis(csum, flat_expert[:, None], axis=1)[:, 0] - 1
    counts = csum[-1]
    padded = (counts + tm - 1) // tm * tm
    padded_ends = jnp.cumsum(padded)
    dest = (padded_ends - padded)[flat_expert] + rank
    n_blocks = -(-n_assign // tm) + E
    row_token = jnp.zeros((n_blocks * tm,), jnp.int32).at[dest].set(
        jnp.arange(n_assign, dtype=jnp.int32) // 2)
    block_expert = jnp.minimum(
        jnp.searchsorted(padded_ends, jnp.arange(n_blocks, dtype=jnp.int32) * tm, side='right'),
        E - 1).astype(jnp.int32) + index * E
    n_active = (padded_ends[-1:] // tm).astype(jnp.int32)
    rows = gather_rows(h, row_token, n_active, tm=tm)
    act = ffn_up(rows, block_expert, n_active, w_gate, w_up, tm=tm, tf=1024)
    sub = tm // tm_down
    y_rows = ffn_down(act, jnp.repeat(block_expert, sub), n_active * sub, w_down, tm=tm_down, tn=512)
    return combine_rows(x, y_rows, dest.astype(jnp.int32), gates, final_gain)


def kernel(x, norm_mix, w_in, lb_logits, hgrn_norm, w_out, norm_ffn, dense_w_gate, dense_w_up,
           dense_w_down, moe_router, moe_w_gate, moe_w_up, moe_w_down, final_norm):
    B, S, D = x.shape
    depth = w_in.shape[0]
    lb_sm = jax.nn.softmax(lb_logits.astype(F32), axis=0)
    lower_bounds = jnp.cumsum(lb_sm, axis=0) - lb_sm[0:1]
    xt = x.reshape(B * S, D)
    for layer in range(depth):
        proj = norm_matmul(xt, norm_mix[layer], w_in[layer].astype(BF16))
        o_hgrn = hgrn2(proj, lower_bounds[layer], hgrn_norm[layer], batch=B, seq=S)
        o_attn = dilated_attention(proj, batch=B, seq=S, col0=4 * HGRN_WIDTH)
        i = layer // 2
        last = layer == depth - 1
        if layer % 2 == 0:
            xt, h = out_proj_norm(o_hgrn, o_attn, xt, w_out[layer].astype(BF16), norm_ffn[layer])
            xt = dense_ffn(h, xt, dense_w_gate, dense_w_up, dense_w_down, i)
            if last:
                xt = rms_rows(xt, final_norm)
        else:
            xt, h, gates, idx = out_proj_norm(o_hgrn, o_attn, xt, w_out[layer].astype(BF16),
                                              norm_ffn[layer], moe_router[i])
            xt = moe_layer(h, xt, gates, idx, moe_w_gate, moe_w_up, moe_w_down, i,
                           final_norm if last else None)
    return xt.reshape(B, S, D)
```

```python
import functools

import jax
import jax.numpy as jnp
from jax import lax
from jax.experimental import pallas as pl
from jax.experimental.pallas import tpu as pltpu

F32 = jnp.float32
BF16 = jnp.bfloat16

HEAD_DIM = 128
N_HGRN_HEADS = 8
N_ATTN_HEADS = 8
HGRN_WIDTH = N_HGRN_HEADS * HEAD_DIM
ATTN_WIDTH = N_ATTN_HEADS * HEAD_DIM
HGRN_CHUNK = 64
HGRN_SUB = 8
HGRN_HEADS_PER_STEP = 8
DILATED_PATTERNS = ((128, 1), (512, 4), (2048, 16))
ATTN_CLASSES = 4
ATTN_PAD_N = 128
ATTN_PAD_M = 512
ATTN_UNITS = 16
OUT_PROJ_SUB_ROWS = 128
ROPE_THETA = 10000.0
N_EXPERTS = 8
NORM_EPS = 1e-6
VMEM_LIMIT = 56 * 1024 * 1024

_NT = (((1,), (1,)), ((), ()))
_TN = (((0,), (0,)), ((), ()))


def _params(semantics):
    return pltpu.CompilerParams(dimension_semantics=semantics, vmem_limit_bytes=VMEM_LIMIT)


def _silu(z):
    return z * jax.nn.sigmoid(z)


def _rms(z, gain):
    ms = jnp.mean(z * z, axis=-1, keepdims=True)
    return z * lax.rsqrt(ms + NORM_EPS) * gain


def _in_proj_kernel(x_ref, g_ref, w_ref, o_ref, h_ref):
    @pl.when(pl.program_id(1) == 0)
    def _():
        h_ref[...] = _rms(x_ref[...], g_ref[...]).astype(BF16)

    o_ref[...] = jnp.dot(h_ref[...], w_ref[...].astype(BF16),
                         preferred_element_type=F32).astype(o_ref.dtype)


def norm_matmul(x, gain, w, index, *, tm=1024, tn=1024):
    T, D = x.shape
    N = w.shape[2]
    tm, tn = min(tm, T), min(tn, N)
    return pl.pallas_call(
        _in_proj_kernel,
        grid=(T // tm, N // tn),
        in_specs=[pl.BlockSpec((tm, D), lambda i, j: (i, 0)),
                  pl.BlockSpec((1, D), lambda i, j: (0, 0)),
                  pl.BlockSpec((None, D, tn), lambda i, j: (index, 0, j))],
        out_specs=pl.BlockSpec((tm, tn), lambda i, j: (i, j)),
        out_shape=jax.ShapeDtypeStruct((T, N), BF16),
        scratch_shapes=[pltpu.VMEM((tm, D), BF16)],
        compiler_params=_params(("parallel", "arbitrary")),
        name="in_proj",
    )(x, gain.reshape(1, D).astype(F32), w)


def _hgrn_kernel(q_ref, f_ref, i_ref, g_ref, lb_ref, gain_ref, o_ref,
                 st_ref, b_ref, k_ref, v_ref, *, n_chunks, heads):
    C, SUB = HGRN_CHUNK, HGRN_SUB

    @pl.when(pl.program_id(2) == 0)
    def _():
        st_ref[...] = jnp.zeros_like(st_ref)

    row = lax.broadcasted_iota(jnp.int32, (C, C), 0)
    col = lax.broadcasted_iota(jnp.int32, (C, C), 1)
    tril = (col <= row).astype(BF16)
    ones = jnp.ones((HEAD_DIM, HEAD_DIM), BF16)
    sub_row = lax.broadcasted_iota(jnp.int32, (SUB, HEAD_DIM), 0)

    pairs = []
    span = C
    while span > SUB:
        half = span // 2
        pairs += [(base + half, base + span, base, base + half) for base in range(0, C, span)]
        span = half

    def gates(hh, rows):
        lanes = slice(hh * HEAD_DIM, (hh + 1) * HEAD_DIM)
        lb = lb_ref[:, lanes]
        f = lb + (1.0 - lb) * jax.nn.sigmoid(f_ref[rows, lanes].astype(F32))
        g = jnp.log(f)
        g_hi = g.astype(BF16)
        g_lo = (g - g_hi.astype(F32)).astype(BF16)
        b = (jnp.dot(tril, g_hi, preferred_element_type=F32)
             + jnp.dot(tril, g_lo, preferred_element_type=F32))
        return dict(lanes=lanes, b=b, kk=1.0 - f, q=_silu(q_ref[rows, lanes].astype(F32)),
                    v=i_ref[rows, lanes].astype(F32))

    def scores(hh, s):
        b, kk, q, v = s["b"], s["kk"], s["q"], s["v"]
        b_scr, k_scr = b_ref.at[hh], k_ref.at[hh]
        b_scr[...] = b
        k_scr[...] = kk
        v_ref[hh] = v
        s["v16"] = v.astype(BF16)
        b_end = b[C - 1:C, :]
        st = st_ref[hh]
        s["o"] = lax.dot_general((q * jnp.exp(b)).astype(BF16), st.astype(BF16), _NT,
                                 preferred_element_type=F32)
        kd = (kk * jnp.exp(b_end - b)).astype(BF16)
        st_ref[hh] = st * jnp.exp(b_end) + lax.dot_general(s["v16"], kd, _TN,
                                                           preferred_element_type=F32)
        s["a_off"] = []
        for t0, t1, s0, s1 in pairs:
            b_mid = b[s1 - 1:s1, :]
            qa = (q[t0:t1] * jnp.exp(b[t0:t1] - b_mid)).astype(BF16)
            ka = (kk[s0:s1] * jnp.exp(b_mid - b[s0:s1])).astype(BF16)
            s["a_off"].append(lax.dot_general(qa, ka, _NT, preferred_element_type=F32))
        ps = []
        for r in range(0, C, SUB):
            qs = q[r:r + SUB]
            bs = b[r:r + SUB]
            for j in range(SUB):
                b_row = jnp.broadcast_to(b_scr[r + j:r + j + 1, :], (SUB, HEAD_DIM))
                k_row = jnp.broadcast_to(k_scr[r + j:r + j + 1, :], (SUB, HEAD_DIM))
                d = bs - b_row if j == 0 else jnp.where(sub_row >= j, bs - b_row, -jnp.inf)
                ps.append(qs * k_row * jnp.exp(d))
        s["a_diag"] = jnp.dot(jnp.concatenate(ps, axis=0).astype(BF16), ones,
                              preferred_element_type=F32)

    def values(hh, s):
        v_scr = v_ref.at[hh]
        parts = [None] * (C // SUB)

        def add(idx, piece):
            parts[idx] = piece if parts[idx] is None else parts[idx] + piece

        for (t0, t1, s0, s1), a in zip(pairs, s["a_off"]):
            blk = jnp.dot(a.astype(BF16), s["v16"][s0:s1], preferred_element_type=F32)
            for j in range((t1 - t0) // SUB):
                add(t0 // SUB + j, blk[j * SUB:(j + 1) * SUB])
        a = s["a_diag"]
        for r in range(0, C, SUB):
            for j in range(SUB):
                v_row = jnp.broadcast_to(v_scr[r + j:r + j + 1, :], (SUB, HEAD_DIM))
                add(r // SUB, a[(r + j) * SUB:(r + j + 1) * SUB] * v_row)
        s["o"] = s["o"] + jnp.concatenate(parts, axis=0)

    def finish(s, rows):
        lanes = s["lanes"]
        o = _rms(s["o"], gain_ref[:, lanes]) * _silu(g_ref[rows, lanes].astype(F32))
        o_ref[rows, lanes] = o.astype(o_ref.dtype)

    def chunk(ci, carry):
        rows = pl.ds(pl.multiple_of(ci * C, C), C)
        state = [gates(hh, rows) for hh in range(heads)]
        for hh in range(heads):
            scores(hh, state[hh])
        for hh in range(heads):
            values(hh, state[hh])
        for hh in range(heads):
            finish(state[hh], rows)
        return carry

    lax.fori_loop(0, n_chunks, chunk, 0)


def hgrn2(proj, lower_bound, out_gain, *, batch, seq, tt=512, heads=HGRN_HEADS_PER_STEP):
    T = proj.shape[0]
    H = N_HGRN_HEADS // heads
    width = heads * HEAD_DIM
    tt = min(tt, seq)
    nt = seq // tt

    def sec(k):
        return pl.BlockSpec((tt, width), lambda b, h, c, k=k: (b * nt + c, k * H + h))

    vec = pl.BlockSpec((1, width), lambda b, h, c: (0, h))
    per_head = pltpu.VMEM((heads, HGRN_CHUNK, HEAD_DIM), F32)
    return pl.pallas_call(
        functools.partial(_hgrn_kernel, n_chunks=tt // HGRN_CHUNK, heads=heads),
        grid=(batch, H, nt),
        in_specs=[sec(0), sec(1), sec(2), sec(3), vec, vec],
        out_specs=pl.BlockSpec((tt, width), lambda b, h, c: (b * nt + c, h)),
        out_shape=jax.ShapeDtypeStruct((T, HGRN_WIDTH), BF16),
        scratch_shapes=[pltpu.VMEM((heads, HEAD_DIM, HEAD_DIM), F32), per_head, per_head, per_head],
        compiler_params=_params(("parallel", "parallel", "arbitrary")),
        name="hgrn2",
    )(proj, proj, proj, proj,
      lower_bound.reshape(1, HGRN_WIDTH).astype(F32), out_gain.reshape(1, HGRN_WIDTH).astype(F32))


def _attn_kernel(q_ref, k_ref, v_ref, cos_ref, sin_ref, o_ref,
                 qn, kn, vn, qm, km, vm, o1, o2, o3, l1, l2, l3, bias_ref, *, seq):
    BLK = 128
    ROWS = 512
    NC = ATTN_CLASSES
    per_class = seq // NC
    scale = HEAD_DIM ** -0.5

    def rope(i, carry):
        rows = pl.ds(pl.multiple_of(i * ROWS, ROWS), ROWS)
        c = cos_ref[rows, :]
        s = sin_ref[rows, :]
        q = q_ref[rows, :].astype(F32)
        k = k_ref[rows, :].astype(F32)
        qn[rows, :] = (q * c + pltpu.roll(q, HEAD_DIM // 2, 1) * s) * scale
        padded = pl.ds(pl.multiple_of(i * ROWS, ROWS) + ATTN_PAD_N, ROWS)
        kn[padded, :] = k * c + pltpu.roll(k, HEAD_DIM // 2, 1) * s
        vn[padded, :] = v_ref[rows, :].astype(F32)
        return carry

    lax.fori_loop(0, seq // ROWS, rope, 0)

    kn[0:ATTN_PAD_N, :] = jnp.zeros((ATTN_PAD_N, HEAD_DIM), F32)
    vn[0:ATTN_PAD_N, :] = jnp.zeros((ATTN_PAD_N, HEAD_DIM), F32)
    for c in range(NC):
        base = c * (ATTN_PAD_M + per_class)
        km[base:base + ATTN_PAD_M, :] = jnp.zeros((ATTN_PAD_M, HEAD_DIM), F32)
        vm[base:base + ATTN_PAD_M, :] = jnp.zeros((ATTN_PAD_M, HEAD_DIM), F32)

    def regroup(i, carry):
        j0 = pl.multiple_of(i * ROWS, ROWS)
        for c in range(NC):
            src = pl.ds(NC * j0 + c, ROWS, stride=NC)
            qm[pl.ds(c * per_class + j0, ROWS), :] = qn[src, :]
            dst = pl.ds(c * (ATTN_PAD_M + per_class) + ATTN_PAD_M + j0, ROWS)
            km[dst, :] = kn[pl.ds(NC * j0 + c + ATTN_PAD_N, ROWS, stride=NC), :]
            vm[dst, :] = vn[pl.ds(NC * j0 + c + ATTN_PAD_N, ROWS, stride=NC), :]
        return carry

    lax.fori_loop(0, per_class // ROWS, regroup, 0)

    qi = lax.broadcasted_iota(jnp.int32, (BLK, 2 * BLK), 0)
    kj = lax.broadcasted_iota(jnp.int32, (BLK, 2 * BLK), 1)
    band = (kj >= qi) & (kj <= qi + BLK)
    bias_ref[0] = jnp.where(band & (kj >= BLK), 0.0, -jnp.inf)
    bias_ref[1] = jnp.where(band, 0.0, -jnp.inf)
    ones = jnp.ones((2 * BLK, HEAD_DIM), BF16)

    def ds(start, size, stride):
        return pl.ds(start, size) if stride == 1 else pl.ds(start, size, stride=stride)

    def unit_rows(d, u):
        if d == 1:
            return u, pl.ds(u * BLK, BLK), pl.ds(u * BLK + (ATTN_PAD_N - BLK), 2 * BLK), pl.ds(u * BLK, BLK)
        step = d // NC
        nb = per_class // (BLK * step)
        n = u & (nb - 1)
        r = lax.shift_right_logical(u, nb.bit_length() - 1) & (step - 1)
        c = lax.shift_right_logical(u, (nb * step).bit_length() - 1)
        j0 = n * (BLK * step) + r
        return (n, ds(c * per_class + j0, BLK, step),
                ds(c * (ATTN_PAD_M + per_class) + (ATTN_PAD_M - BLK * step) + j0, 2 * BLK, step),
                pl.ds(NC * j0 + c, BLK, stride=d))

    def qk(d, u):
        q_src, k_src, v_src = (qn, kn, vn) if d == 1 else (qm, km, vm)
        n, q_rows, k_rows, out_rows = unit_rows(d, u)
        q = q_src[q_rows, :].astype(BF16)
        k = k_src[k_rows, :].astype(BF16)
        v = v_src[k_rows, :].astype(BF16)
        s = lax.dot_general(q, k, _NT, preferred_element_type=F32)
        return dict(n=n, out_rows=out_rows, s=s, v=v)

    def pv(t):
        s = t["s"] + bias_ref[jnp.minimum(t["n"], 1)]
        t["m"] = jnp.max(s, axis=-1, keepdims=True)
        p = jnp.exp(s - t["m"]).astype(BF16)
        t["acc"] = jnp.dot(p, jnp.concatenate([t["v"], ones], axis=1), preferred_element_type=F32)

    def store(t, o_out, l_out):
        den = t["acc"][:, HEAD_DIM:]
        o_out[t["out_rows"], :] = t["acc"][:, :HEAD_DIM] / den
        l_out[t["out_rows"], :] = t["m"] + jnp.log(den)

    for (window, d), o_out, l_out in zip(DILATED_PATTERNS, (o1, o2, o3), (l1, l2, l3)):
        assert window == BLK * d and (d == 1 or d % NC == 0)
        assert BLK * d <= (ATTN_PAD_N if d == 1 else ATTN_PAD_M * NC) and per_class % (BLK * max(d // NC, 1)) == 0
        n_units = seq // BLK
        assert n_units % ATTN_UNITS == 0

        def units(it, carry, d=d, o_out=o_out, l_out=l_out):
            ts = [qk(d, it * ATTN_UNITS + j) for j in range(ATTN_UNITS)]
            for t in ts:
                pv(t)
            for t in ts:
                store(t, o_out, l_out)
            return carry

        lax.fori_loop(0, n_units // ATTN_UNITS, units, 0)

    def combine(i, carry):
        rows = pl.ds(pl.multiple_of(i * ROWS, ROWS), ROWS)
        a1, a2, a3 = l1[rows, :], l2[rows, :], l3[rows, :]
        m = jnp.maximum(jnp.maximum(a1, a2), a3)
        e1, e2, e3 = jnp.exp(a1 - m), jnp.exp(a2 - m), jnp.exp(a3 - m)
        out = (e1 * o1[rows, :] + e2 * o2[rows, :] + e3 * o3[rows, :]) / (e1 + e2 + e3)
        o_ref[rows, :] = out.astype(o_ref.dtype)
        return carry

    lax.fori_loop(0, seq // ROWS, combine, 0)


def _rope_tables(seq):
    half = HEAD_DIM // 2
    inv_freq = ROPE_THETA ** (-jnp.arange(half, dtype=F32) / half)
    ang = jnp.arange(seq, dtype=F32)[:, None] * inv_freq[None, :]
    cos, sin = jnp.cos(ang), jnp.sin(ang)
    return jnp.concatenate([cos, cos], axis=-1), jnp.concatenate([-sin, sin], axis=-1)


def dilated_attention(proj, *, batch, seq, col0):
    T = proj.shape[0]
    H = N_ATTN_HEADS
    c0 = col0 // HEAD_DIM
    cos, sin = _rope_tables(seq)

    def sec(k):
        return pl.BlockSpec((seq, HEAD_DIM), lambda b, h, k=k: (b, c0 + k * H + h))

    table = pl.BlockSpec((seq, HEAD_DIM), lambda b, h: (0, 0), pipeline_mode=pl.Buffered(1))
    slab = pltpu.VMEM((seq, HEAD_DIM), F32)
    padded_n = pltpu.VMEM((ATTN_PAD_N + seq, HEAD_DIM), F32)
    padded_m = pltpu.VMEM((ATTN_CLASSES * ATTN_PAD_M + seq, HEAD_DIM), F32)
    return pl.pallas_call(
        functools.partial(_attn_kernel, seq=seq),
        grid=(batch, H),
        in_specs=[sec(0), sec(1), sec(2), table, table],
        out_specs=pl.BlockSpec((seq, HEAD_DIM), lambda b, h: (b, h)),
        out_shape=jax.ShapeDtypeStruct((T, ATTN_WIDTH), BF16),
        scratch_shapes=[slab, padded_n, padded_n, slab, padded_m, padded_m] + [slab] * 6
                       + [pltpu.VMEM((2, HEAD_DIM, 2 * HEAD_DIM), F32)],
        compiler_params=_params(("parallel", "parallel")),
        name="dilated_attn",
    )(proj, proj, proj, cos, sin)


def _out_proj_kernel(*refs, route):
    if route:
        oh_ref, oa_ref, x_ref, w_ref, g_ref, wr_ref, xo_ref, h_ref, gate_ref, idx_ref = refs
    else:
        oh_ref, oa_ref, x_ref, w_ref, g_ref, xo_ref, h_ref = refs
    hw = oh_ref.shape[1]
    tm = x_ref.shape[0]
    sub = min(tm, OUT_PROJ_SUB_ROWS)
    for r0 in range(0, tm, sub):
        rows = slice(r0, r0 + sub)
        y = (jnp.dot(oh_ref[rows, :], w_ref[:hw, :], preferred_element_type=F32)
             + jnp.dot(oa_ref[rows, :], w_ref[hw:, :], preferred_element_type=F32))
        x = x_ref[rows, :] + y
        xo_ref[rows, :] = x
        h = _rms(x, g_ref[...])
        h_ref[rows, :] = h.astype(h_ref.dtype)
        if route:
            h_hi = h.astype(BF16)
            h_lo = (h - h_hi.astype(F32)).astype(BF16)
            logits = (jnp.dot(h_hi, wr_ref[0], preferred_element_type=F32)
                      + jnp.dot(h_lo, wr_ref[0], preferred_element_type=F32)
                      + jnp.dot(h_hi, wr_ref[1], preferred_element_type=F32))
            lane = lax.broadcasted_iota(jnp.int32, logits.shape, 1)
            lg = jnp.where(lane < N_EXPERTS, logits, -jnp.inf)
            m1 = jnp.max(lg, axis=-1, keepdims=True)
            i1 = jnp.min(jnp.where(lg == m1, lane, HEAD_DIM), axis=-1, keepdims=True)
            lg = jnp.where(lane == i1, -jnp.inf, lg)
            m2 = jnp.max(lg, axis=-1, keepdims=True)
            i2 = jnp.min(jnp.where(lg == m2, lane, HEAD_DIM), axis=-1, keepdims=True)
            e = jnp.exp(m2 - m1)
            gate_ref[rows, :] = jnp.where(lane == 0, 1.0 / (1.0 + e),
                                          jnp.where(lane == 1, e / (1.0 + e), 0.0))
            idx_ref[rows, :] = jnp.where(lane == 0, i1, jnp.where(lane == 1, i2, 0))


def out_proj_norm(o_hgrn, o_attn, x, w, gain, w_router=None, *, tm=512):
    T, D = x.shape
    tm = min(tm, T)
    route = w_router is not None
    row = lambda width: pl.BlockSpec((tm, width), lambda i: (i, 0))
    whole = lambda a: pl.BlockSpec(a.shape, lambda i: (0, 0), pipeline_mode=pl.Buffered(1))
    gain2 = gain.reshape(1, D).astype(F32)
    args = [o_hgrn, o_attn, x, w, gain2]
    in_specs = [row(o_hgrn.shape[1]), row(o_attn.shape[1]), row(D), whole(w), whole(gain2)]
    out_specs = [row(D), row(D)]
    out_shape = [jax.ShapeDtypeStruct((T, D), F32), jax.ShapeDtypeStruct((T, D), F32 if route else BF16)]
    if route:
        wr = jnp.zeros((D, HEAD_DIM), F32).at[:, :N_EXPERTS].set(w_router.astype(F32))
        wr_hi = wr.astype(BF16)
        wr = jnp.stack([wr_hi, (wr - wr_hi.astype(F32)).astype(BF16)])
        args.append(wr)
        in_specs.append(pl.BlockSpec(wr.shape, lambda i: (0, 0, 0), pipeline_mode=pl.Buffered(1)))
        out_specs += [row(HEAD_DIM), row(HEAD_DIM)]
        out_shape += [jax.ShapeDtypeStruct((T, HEAD_DIM), F32),
                      jax.ShapeDtypeStruct((T, HEAD_DIM), jnp.int32)]
    return pl.pallas_call(
        functools.partial(_out_proj_kernel, route=route),
        grid=(T // tm,),
        in_specs=in_specs,
        out_specs=out_specs,
        out_shape=out_shape,
        compiler_params=_params(("parallel",)),
        name="out_proj_route" if route else "out_proj",
    )(*args)


def _fresh_weights(be_ref, i):
    return (i == 0) | (be_ref[i] != be_ref[jnp.maximum(i - 1, 0)])


def _ffn_up_kernel(be_ref, na_ref, x_ref, wg_ref, wu_ref, a_ref, wg16, wu16):
    i = pl.program_id(1)
    active = i < na_ref[0]

    @pl.when(active & _fresh_weights(be_ref, i))
    def _():
        wg16[...] = wg_ref[...].astype(BF16)
        wu16[...] = wu_ref[...].astype(BF16)

    @pl.when(active)
    def _():
        x = x_ref[...]
        g = jnp.dot(x, wg16[...], preferred_element_type=F32)
        u = jnp.dot(x, wu16[...], preferred_element_type=F32)
        a_ref[...] = (_silu(g) * u).astype(a_ref.dtype)

    @pl.when(jnp.logical_not(active))
    def _():
        a_ref[...] = jnp.zeros_like(a_ref)


def _ffn_down_kernel(*refs, residual):
    if residual:
        be_ref, na_ref, a_ref, wd_ref, x_ref, o_ref, wd16 = refs
    else:
        be_ref, na_ref, a_ref, wd_ref, o_ref, wd16 = refs
    i = pl.program_id(1)
    active = i < na_ref[0]

    @pl.when(active & _fresh_weights(be_ref, i))
    def _():
        wd16[...] = wd_ref[...].astype(BF16)

    @pl.when(active)
    def _():
        y = jnp.dot(a_ref[...], wd16[...], preferred_element_type=F32)
        o_ref[...] = x_ref[...] + y if residual else y

    @pl.when(jnp.logical_not(active))
    def _():
        o_ref[...] = jnp.zeros_like(o_ref)


def _block(i, na):
    return jnp.minimum(i, na[0] - 1)


def ffn_up(rows, block_expert, n_active, w_gate, w_up, *, tm, tf):
    R, D = rows.shape
    F = w_gate.shape[2]
    tf = min(tf, F)
    wspec =pl.BlockSpec((None, D, tf), lambda f, i, be, na: (be[_block(i, na)], 0, f))
    return pl.pallas_call(
        _ffn_up_kernel,
        grid_spec=pltpu.PrefetchScalarGridSpec(
            num_scalar_prefetch=2,
            grid=(F // tf, R // tm),
            in_specs=[pl.BlockSpec((tm, D), lambda f, i, be, na: (_block(i, na), 0)), wspec, wspec],
            out_specs=pl.BlockSpec((tm, tf), lambda f, i, be, na: (i, f)),
            scratch_shapes=[pltpu.VMEM((D, tf), BF16), pltpu.VMEM((D, tf), BF16)]),
        out_shape=jax.ShapeDtypeStruct((R, F), BF16),
        compiler_params=_params(("arbitrary", "arbitrary")),
        name="ffn_up",
    )(block_expert, n_active, rows, w_gate, w_up)


def ffn_down(act, block_expert, n_active, w_down, x=None, *, tm, tn):
    R, F = act.shape
    D = w_down.shape[2]
    tn = min(tn, D)
    residual = x is not None
    tile = pl.BlockSpec((tm, tn), lambda n, i, be, na: (i, n))
    in_specs = [pl.BlockSpec((tm, F), lambda n, i, be, na: (_block(i, na), 0)),
                pl.BlockSpec((None, F, tn), lambda n, i, be, na: (be[_block(i, na)], 0, n))]
    args = [act, w_down]
    if residual:
        in_specs.append(tile)
        args.append(x)
    return pl.pallas_call(
        functools.partial(_ffn_down_kernel, residual=residual),
        grid_spec=pltpu.PrefetchScalarGridSpec(
            num_scalar_prefetch=2,
            grid=(D // tn, R // tm),
            in_specs=in_specs,
            out_specs=tile,
            scratch_shapes=[pltpu.VMEM((F, tn), BF16)]),
        out_shape=jax.ShapeDtypeStruct((R, D), F32),
        compiler_params=_params(("arbitrary", "arbitrary")),
        name="ffn_down",
    )(block_expert, n_active, *args)


def dense_ffn(h, x, w_gate, w_up, w_down, index, *, tm=1024, tf=512, tm_down=512, tn=512):
    T = x.shape[0]

    def blocks(t):
        return jnp.full((T // t,), index, jnp.int32), jnp.full((1,), T // t, jnp.int32)

    act = ffn_up(h, *blocks(tm), w_gate, w_up, tm=tm, tf=tf)
    return ffn_down(act, *blocks(tm_down), w_down, x, tm=tm_down, tn=tn)


def _row_copy(src_hbm, dst_ref, sem, src_row, dst_row):
    return pltpu.make_async_copy(src_hbm.at[pl.ds(src_row, 1), :], dst_ref.at[pl.ds(dst_row, 1), :], sem)


def _gather_rows_kernel(na_ref, idx_ref, nxt_ref, h_hbm, o_ref, buf, sem):
    tm = buf.shape[1]
    i = pl.program_id(0)
    slot = i % 2

    def rows(ids_ref, s, method):
        def body(r, c):
            getattr(_row_copy(h_hbm, buf.at[s], sem.at[s], ids_ref[0, 0, r], r), method)()
            return c

        lax.fori_loop(0, tm, body, 0, unroll=8)

    @pl.when(i == 0)
    def _():
        rows(idx_ref, 0, "start")

    @pl.when(i + 1 < na_ref[0])
    def _():
        rows(nxt_ref, 1 - slot, "start")

    @pl.when(i < na_ref[0])
    def _():
        rows(idx_ref, slot, "wait")
        o_ref[...] = buf[slot].astype(o_ref.dtype)

    @pl.when(i >= na_ref[0])
    def _():
        o_ref[...] = jnp.zeros_like(o_ref)


def gather_rows(h, row_token, n_active, *, tm):
    T, D = h.shape
    R = row_token.shape[0]
    ids = row_token.reshape(R // tm, 1, tm)
    return pl.pallas_call(
        _gather_rows_kernel,
        grid_spec=pltpu.PrefetchScalarGridSpec(
            num_scalar_prefetch=1,
            grid=(R // tm,),
            in_specs=[pl.BlockSpec((1, 1, tm), lambda i, na: (_block(i, na), 0, 0),
                                   memory_space=pltpu.SMEM),
                      pl.BlockSpec((1, 1, tm), lambda i, na: (_block(i + 1, na), 0, 0),
                                   memory_space=pltpu.SMEM),
                      pl.BlockSpec(memory_space=pl.ANY)],
            out_specs=pl.BlockSpec((tm, D), lambda i, na: (i, 0)),
            scratch_shapes=[pltpu.VMEM((2, tm, D), F32), pltpu.SemaphoreType.DMA((2,))]),
        out_shape=jax.ShapeDtypeStruct((R, D), BF16),
        compiler_params=_params(("arbitrary",)),
        name="gather_rows",
    )(n_active, ids, ids, h)


def _combine_kernel(*refs, final):
    if final:
        dest_ref, x_ref, gate_ref, gain_ref, y_hbm, o_ref, buf0, buf1, sem = refs
    else:
        dest_ref, x_ref, gate_ref, y_hbm, o_ref, buf0, buf1, sem = refs
    tm = buf0.shape[0]

    def copies(r):
        return (_row_copy(y_hbm, buf0, sem.at[0], dest_ref[0, 0, 2 * r], r),
                _row_copy(y_hbm, buf1, sem.at[1], dest_ref[0, 0, 2 * r + 1], r))

    def start(r, c):
        for cp in copies(r):
            cp.start()
        return c

    def wait(r, c):
        for cp in copies(r):
            cp.wait()
        return c

    lax.fori_loop(0, tm, start, 0, unroll=8)
    lax.fori_loop(0, tm, wait, 0, unroll=8)
    gates = gate_ref[...]
    x = x_ref[...] + gates[:, 0:1] * buf0[...] + gates[:, 1:2] * buf1[...]
    o_ref[...] = _rms(x, gain_ref[...]) if final else x


def combine_rows(x, y_rows, dest, gates, final_gain=None, *, tm=256):
    T, D = x.shape
    final = final_gain is not None
    row = pl.BlockSpec((tm, D), lambda i: (i, 0))
    in_specs = [pl.BlockSpec((1, 1, 2 * tm), lambda i: (i, 0, 0), memory_space=pltpu.SMEM),
                row, pl.BlockSpec((tm, gates.shape[1]), lambda i: (i, 0))]
    args = [dest.reshape(T // tm, 1, 2 * tm), x, gates]
    if final:
        in_specs.append(pl.BlockSpec((1, D), lambda i: (0, 0)))
        args.append(final_gain.reshape(1, D).astype(F32))
    in_specs.append(pl.BlockSpec(memory_space=pl.ANY))
    args.append(y_rows)
    return pl.pallas_call(
        functools.partial(_combine_kernel, final=final),
        grid=(T // tm,),
        in_specs=in_specs,
        out_specs=row,
        out_shape=jax.ShapeDtypeStruct((T, D), F32),
        scratch_shapes=[pltpu.VMEM((tm, D), F32), pltpu.VMEM((tm, D), F32),
                        pltpu.SemaphoreType.DMA((2,))],
        compiler_params=_params(("arbitrary",)),
        name="combine_rows",
    )(*args)


def _rms_rows_kernel(x_ref, g_ref, o_ref):
    o_ref[...] = _rms(x_ref[...], g_ref[...])


def rms_rows(x, gain, *, tm=512):
    T, D = x.shape
    tm = min(tm, T)
    return pl.pallas_call(
        _rms_rows_kernel,
        grid=(T // tm,),
        in_specs=[pl.BlockSpec((tm, D), lambda i: (i, 0)), pl.BlockSpec((1, D), lambda i: (0, 0))],
        out_specs=pl.BlockSpec((tm, D), lambda i: (i, 0)),
        out_shape=jax.ShapeDtypeStruct((T, D), F32),
        compiler_params=_params(("parallel",)),
        name="final_norm",
    )(x, gain.reshape(1, D).astype(F32))


def moe_layer(h, x, gates, top_idx, w_gate, w_up, w_down, index, final_gain=None, *,
              tm=512, tm_down=256):
    T, D = x.shape
    E = w_gate.shape[1]
    w_gate, w_up, w_down = (w.reshape((-1,) + w.shape[2:]) for w in (w_gate, w_up, w_down))
    n_assign = 2 * T
    flat_expert = top_idx[:, :2].reshape(n_assign)
    onehot = (flat_expert[:, None] == jnp.arange(E, dtype=jnp.int32)[None, :]).astype(jnp.int32)
    csum = jnp.cumsum(onehot, axis=0)
    rank = jnp.take_along_axis(csum, flat_expert[:, None], axis=1)[:, 0] - 1
    counts = csum[-1]
    padded = (counts + tm - 1) // tm * tm
    padded_ends = jnp.cumsum(padded)
    dest = (padded_ends - padded)[flat_expert] + rank
    n_blocks = -(-n_assign // tm) + E
    row_token = jnp.zeros((n_blocks * tm,), jnp.int32).at[dest].set(
        jnp.arange(n_assign, dtype=jnp.int32) // 2)
    block_expert = jnp.minimum(
        jnp.searchsorted(padded_ends, jnp.arange(n_blocks, dtype=jnp.int32) * tm, side='right'),
        E - 1).astype(jnp.int32) + index * E
    n_active = (padded_ends[-1:] // tm).astype(jnp.int32)
    rows = gather_rows(h, row_token, n_active, tm=tm)
    act = ffn_up(rows, block_expert, n_active, w_gate, w_up, tm=tm, tf=1024)
    sub = tm // tm_down
    y_rows = ffn_down(act, jnp.repeat(block_expert, sub), n_active * sub, w_down, tm=tm_down, tn=512)
    return combine_rows(x, y_rows, dest.astype(jnp.int32), gates, final_gain)


def kernel(x, norm_mix, w_in, lb_logits, hgrn_norm, w_out, norm_ffn, dense_w_gate, dense_w_up,
           dense_w_down, moe_router, moe_w_gate, moe_w_up, moe_w_down, final_norm):
    B, S, D = x.shape
    depth = w_in.shape[0]
    lb_sm = jax.nn.softmax(lb_logits.astype(F32), axis=0)
    lower_bounds = jnp.cumsum(lb_sm, axis=0) - lb_sm[0:1]
    xt = x.reshape(B * S, D)
    for layer in range(depth):
        proj = norm_matmul(xt, norm_mix[layer], w_in, layer)
        o_hgrn = hgrn2(proj, lower_bounds[layer], hgrn_norm[layer], batch=B, seq=S)
        o_attn = dilated_attention(proj, batch=B, seq=S, col0=4 * HGRN_WIDTH)
        i = layer // 2
        last = layer == depth - 1
        if layer % 2 == 0:
            xt, h = out_proj_norm(o_hgrn, o_attn, xt, w_out[layer].astype(BF16), norm_ffn[layer])
            xt = dense_ffn(h, xt, dense_w_gate, dense_w_up, dense_w_down, i)
            if last:
                xt = rms_rows(xt, final_norm)
        else:
            xt, h, gates, idx = out_proj_norm(o_hgrn, o_attn, xt, w_out[layer].astype(BF16),
                                              norm_ffn[layer], moe_router[i])
            xt = moe_layer(h, xt, gates, idx, moe_w_gate, moe_w_up, moe_w_down, i,
                           final_norm if last else None)
    return xt.reshape(B, S, D)
```

```python
import functools
import math

import jax
import jax.numpy as jnp
from jax import lax
from jax.experimental import pallas as pl
from jax.experimental.pallas import tpu as pltpu

F32 = jnp.float32
BF16 = jnp.bfloat16

HEAD_DIM = 128
N_HGRN_HEADS = 8
N_ATTN_HEADS = 8
HGRN_WIDTH = N_HGRN_HEADS * HEAD_DIM
ATTN_WIDTH = N_ATTN_HEADS * HEAD_DIM
HGRN_CHUNK = 64
HGRN_SUB = 8
HGRN_HEADS_PER_STEP = 8
DILATED_PATTERNS = ((128, 1), (512, 4), (2048, 16))
ATTN_CLASSES = 4
ATTN_PAD_N = 128
ATTN_PAD_M = 512
ATTN_UNITS = 16
DMA_UNROLL = 8
OUT_PROJ_SUB_ROWS = 128
ROPE_THETA = 10000.0
N_EXPERTS = 8
NORM_EPS = 1e-6
VMEM_LIMIT = 60 * 1024 * 1024

_NT = (((1,), (1,)), ((), ()))
_TN = (((0,), (0,)), ((), ()))


def _params(semantics):
    return pltpu.CompilerParams(dimension_semantics=semantics, vmem_limit_bytes=VMEM_LIMIT)


def _silu(z):
    return z * jax.nn.sigmoid(z)


def _rms(z, gain):
    ms = jnp.mean(z * z, axis=-1, keepdims=True)
    return z * lax.rsqrt(ms + NORM_EPS) * gain


def _in_proj_kernel(x_ref, g_ref, w_ref, o_ref, h_ref):
    @pl.when(pl.program_id(1) == 0)
    def _():
        h_ref[...] = _rms(x_ref[...], g_ref[...]).astype(BF16)

    o_ref[...] = jnp.dot(h_ref[...], w_ref[...].astype(BF16),
                         preferred_element_type=F32).astype(o_ref.dtype)


def norm_matmul(x, gain, w, index, *, tm=1024, tn=1024):
    T, D = x.shape
    N = w.shape[2]
    tm, tn = min(tm, T), min(tn, N)
    return pl.pallas_call(
        _in_proj_kernel,
        grid=(T // tm, N // tn),
        in_specs=[pl.BlockSpec((tm, D), lambda i, j: (i, 0)),
                  pl.BlockSpec((1, D), lambda i, j: (0, 0)),
                  pl.BlockSpec((None, D, tn), lambda i, j: (index, 0, j))],
        out_specs=pl.BlockSpec((tm, tn), lambda i, j: (i, j)),
        out_shape=jax.ShapeDtypeStruct((T, N), BF16),
        scratch_shapes=[pltpu.VMEM((tm, D), BF16)],
        compiler_params=_params(("parallel", "arbitrary")),
        name="in_proj",
    )(x, gain.reshape(1, D).astype(F32), w)


def _hgrn_kernel(q_ref, f_ref, i_ref, g_ref, lb_ref, gain_ref, o_ref,
                 st_ref, k_ref, tril_ref, select_ref, causal_ref, *, n_chunks, heads):
    C, SUB = HGRN_CHUNK, HGRN_SUB

    @pl.when(pl.program_id(2) == 0)
    def _():
        st_ref[...] = jnp.zeros_like(st_ref)
        row = lax.broadcasted_iota(jnp.int32, (C, C), 0)
        col = lax.broadcasted_iota(jnp.int32, (C, C), 1)
        tril_ref[...] = (col <= row).astype(BF16)
        group = lax.shift_right_logical(
            lax.broadcasted_iota(jnp.int32, (SUB * HEAD_DIM, HEAD_DIM), 0), HEAD_DIM.bit_length() - 1)
        select_ref[...] = (group == lax.broadcasted_iota(jnp.int32, (SUB * HEAD_DIM, HEAD_DIM), 1)
                           ).astype(BF16)
        sub_row = lax.broadcasted_iota(jnp.int32, (SUB, SUB, HEAD_DIM), 1)
        key = lax.broadcasted_iota(jnp.int32, (SUB, SUB, HEAD_DIM), 0)
        causal_ref[...] = jnp.where(sub_row >= key, 0.0, -jnp.inf)

    pairs = []
    span = C
    while span > SUB:
        half = span // 2
        pairs += [(base + half, base + span, base, base + half) for base in range(0, C, span)]
        span = half

    def gates(hh, rows):
        lanes = slice(hh * HEAD_DIM, (hh + 1) * HEAD_DIM)
        lb = lb_ref[:, lanes]
        f = lb + (1.0 - lb) * jax.nn.sigmoid(f_ref[rows, lanes].astype(F32))
        g = jnp.log2(f)
        g_hi = g.astype(BF16)
        g_lo = (g - g_hi.astype(F32)).astype(BF16)
        tril = tril_ref[...]
        b = (jnp.dot(tril, g_hi, preferred_element_type=F32)
             + jnp.dot(tril, g_lo, preferred_element_type=F32))
        return dict(lanes=lanes, b=b, kk=1.0 - f, q=_silu(q_ref[rows, lanes].astype(F32)),
                    v=i_ref[rows, lanes].astype(F32))

    def scores(hh, s):
        b, kk, q, v = s["b"], s["kk"], s["q"], s["v"]
        k_scr = k_ref.at[hh]
        k_scr[...] = jnp.log2(kk) - b
        s["v16"] = v.astype(BF16)
        b_end = b[C - 1:C, :]
        st = st_ref[hh]
        s["o"] = lax.dot_general((q * jnp.exp2(b)).astype(BF16), st.astype(BF16), _NT,
                                 preferred_element_type=F32)
        kd = (kk * jnp.exp2(b_end - b)).astype(BF16)
        st_ref[hh] = st * jnp.exp2(b_end) + lax.dot_general(s["v16"], kd, _TN,
                                                            preferred_element_type=F32)
        s["a_off"] = []
        for t0, t1, s0, s1 in pairs:
            b_mid = b[s1 - 1:s1, :]
            qa = (q[t0:t1] * jnp.exp2(b[t0:t1] - b_mid)).astype(BF16)
            ka = (kk[s0:s1] * jnp.exp2(b_mid - b[s0:s1])).astype(BF16)
            s["a_off"].append(lax.dot_general(qa, ka, _NT, preferred_element_type=F32))
        blocks = []
        for r in range(0, C, SUB):
            qs = q[r:r + SUB]
            bs = b[r:r + SUB]
            groups = []
            for j in range(SUB):
                d = bs + jnp.broadcast_to(k_scr[r + j:r + j + 1, :], (SUB, HEAD_DIM))
                if j:
                    d = d + causal_ref[j]
                groups.append(qs * jnp.exp2(d))
            blocks.append(jnp.concatenate(groups, axis=1))
        a = jnp.dot(jnp.concatenate(blocks, axis=0).astype(BF16), select_ref[...],
                    preferred_element_type=F32)
        s["a_diag"] = jnp.concatenate(
            [a[r:r + SUB] if r == 0 else pltpu.roll(a[r:r + SUB], r, 1) for r in range(0, C, SUB)], axis=0)

    def values(hh, s):
        parts = [jnp.zeros((SUB, HEAD_DIM), F32)] + [None] * (C // SUB - 1)

        def add(idx, piece):
            parts[idx] = piece if parts[idx] is None else parts[idx] + piece

        for (t0, t1, s0, s1), a in zip(pairs, s["a_off"]):
            blk = jnp.dot(a.astype(BF16), s["v16"][s0:s1], preferred_element_type=F32)
            for j in range((t1 - t0) // SUB):
                add(t0 // SUB + j, blk[j * SUB:(j + 1) * SUB])
        v_rows = jnp.concatenate([s["v16"], jnp.zeros((HEAD_DIM - C, HEAD_DIM), BF16)], axis=0)
        diag = jnp.dot(s["a_diag"].astype(BF16), v_rows, preferred_element_type=F32)
        s["o"] = s["o"] + diag + jnp.concatenate(parts, axis=0)

    def finish(s, rows):
        lanes = s["lanes"]
        o = _rms(s["o"], gain_ref[:, lanes]) * _silu(g_ref[rows, lanes].astype(F32))
        o_ref[rows, lanes] = o.astype(o_ref.dtype)

    def chunk(ci, carry):
        rows = pl.ds(pl.multiple_of(ci * C, C), C)
        state = [gates(hh, rows) for hh in range(heads)]
        for hh in range(heads):
            scores(hh, state[hh])
        for hh in range(heads):
            values(hh, state[hh])
        for hh in range(heads):
            finish(state[hh], rows)
        return carry

    lax.fori_loop(0, n_chunks, chunk, 0)


def hgrn2(proj, lower_bound, out_gain, *, batch, seq, tt=512, heads=HGRN_HEADS_PER_STEP):
    T = proj.shape[0]
    H = N_HGRN_HEADS // heads
    width = heads * HEAD_DIM
    tt = min(tt, seq)
    nt = seq // tt

    def sec(k):
        return pl.BlockSpec((tt, width), lambda b, h, c, k=k: (b * nt + c, k * H + h))

    vec = pl.BlockSpec((1, width), lambda b, h, c: (0, h))
    per_head = pltpu.VMEM((heads, HGRN_CHUNK, HEAD_DIM), F32)
    return pl.pallas_call(
        functools.partial(_hgrn_kernel, n_chunks=tt // HGRN_CHUNK, heads=heads),
        grid=(batch, H, nt),
        in_specs=[sec(0), sec(1), sec(2), sec(3), vec, vec],
        out_specs=pl.BlockSpec((tt, width), lambda b, h, c: (b * nt + c, h)),
        out_shape=jax.ShapeDtypeStruct((T, HGRN_WIDTH), BF16),
        scratch_shapes=[pltpu.VMEM((heads, HEAD_DIM, HEAD_DIM), F32), per_head,
                        pltpu.VMEM((HGRN_CHUNK, HGRN_CHUNK), BF16),
                        pltpu.VMEM((HGRN_SUB * HEAD_DIM, HEAD_DIM), BF16),
                        pltpu.VMEM((HGRN_SUB, HGRN_SUB, HEAD_DIM), F32)],
        compiler_params=_params(("parallel", "parallel", "arbitrary")),
        name="hgrn2",
    )(proj, proj, proj, proj,
      lower_bound.reshape(1, HGRN_WIDTH).astype(F32), out_gain.reshape(1, HGRN_WIDTH).astype(F32))


def _attn_kernel(q_ref, k_ref, v_ref, cos_ref, sin_ref, o_ref,
                 qn, kn, vn, qm, km, vm, o1, o2, o3, l1, l2, l3, bias_ref, *, seq):
    BLK = 128
    ROWS = 512
    NC = ATTN_CLASSES
    per_class = seq // NC
    scale = HEAD_DIM ** -0.5 * math.log2(math.e)

    def rope(i, carry):
        rows = pl.ds(pl.multiple_of(i * ROWS, ROWS), ROWS)
        c = cos_ref[rows, :]
        s = sin_ref[rows, :]
        q = q_ref[rows, :].astype(F32)
        k = k_ref[rows, :].astype(F32)
        qn[rows, :] = (q * c + pltpu.roll(q, HEAD_DIM // 2, 1) * s) * scale
        padded = pl.ds(pl.multiple_of(i * ROWS, ROWS) + ATTN_PAD_N, ROWS)
        kn[padded, :] = k * c + pltpu.roll(k, HEAD_DIM // 2, 1) * s
        vn[padded, :] = v_ref[rows, :].astype(F32)
        return carry

    lax.fori_loop(0, seq // ROWS, rope, 0)

    kn[0:ATTN_PAD_N, :] = jnp.zeros((ATTN_PAD_N, HEAD_DIM), F32)
    vn[0:ATTN_PAD_N, :] = jnp.zeros((ATTN_PAD_N, HEAD_DIM), F32)
    for c in range(NC):
        base = c * (ATTN_PAD_M + per_class)
        km[base:base + ATTN_PAD_M, :] = jnp.zeros((ATTN_PAD_M, HEAD_DIM), F32)
        vm[base:base + ATTN_PAD_M, :] = jnp.zeros((ATTN_PAD_M, HEAD_DIM), F32)

    def regroup(i, carry):
        j0 = pl.multiple_of(i * ROWS, ROWS)
        for c in range(NC):
            src = pl.ds(NC * j0 + c, ROWS, stride=NC)
            qm[pl.ds(c * per_class + j0, ROWS), :] = qn[src, :]
            dst = pl.ds(c * (ATTN_PAD_M + per_class) + ATTN_PAD_M + j0, ROWS)
            km[dst, :] = kn[pl.ds(NC * j0 + c + ATTN_PAD_N, ROWS, stride=NC), :]
            vm[dst, :] = vn[pl.ds(NC * j0 + c + ATTN_PAD_N, ROWS, stride=NC), :]
        return carry

    lax.fori_loop(0, per_class // ROWS, regroup, 0)

    qi = lax.broadcasted_iota(jnp.int32, (BLK, 2 * BLK), 0)
    kj = lax.broadcasted_iota(jnp.int32, (BLK, 2 * BLK), 1)
    band = (kj >= qi) & (kj <= qi + BLK)
    bias_ref[0] = jnp.where(band & (kj >= BLK), 0.0, -jnp.inf)
    bias_ref[1] = jnp.where(band, 0.0, -jnp.inf)
    ones = jnp.ones((2 * BLK, HEAD_DIM), BF16)

    def ds(start, size, stride):
        return pl.ds(start, size) if stride == 1 else pl.ds(start, size, stride=stride)

    def unit_rows(d, u):
        if d == 1:
            return u, pl.ds(u * BLK, BLK), pl.ds(u * BLK + (ATTN_PAD_N - BLK), 2 * BLK), pl.ds(u * BLK, BLK)
        step = d // NC
        nb = per_class // (BLK * step)
        n = u & (nb - 1)
        r = lax.shift_right_logical(u, nb.bit_length() - 1) & (step - 1)
        c = lax.shift_right_logical(u, (nb * step).bit_length() - 1)
        j0 = n * (BLK * step) + r
        return (n, ds(c * per_class + j0, BLK, step),
                ds(c * (ATTN_PAD_M + per_class) + (ATTN_PAD_M - BLK * step) + j0, 2 * BLK, step),
                pl.ds(NC * j0 + c, BLK, stride=d))

    def qk(d, u):
        q_src, k_src, v_src = (qn, kn, vn) if d == 1 else (qm, km, vm)
        n, q_rows, k_rows, out_rows = unit_rows(d, u)
        q = q_src[q_rows, :].astype(BF16)
        k = k_src[k_rows, :].astype(BF16)
        v = v_src[k_rows, :].astype(BF16)
        s = lax.dot_general(q, k, _NT, preferred_element_type=F32)
        return dict(n=n, out_rows=out_rows, s=s, v=v)

    def pv(t):
        s = t["s"] + bias_ref[jnp.minimum(t["n"], 1)]
        t["m"] = jnp.max(s, axis=-1, keepdims=True)
        p = jnp.exp2(s - t["m"]).astype(BF16)
        t["acc"] = jnp.dot(p, jnp.concatenate([t["v"], ones], axis=1), preferred_element_type=F32)

    def store(t, o_out, l_out):
        den = t["acc"][:, HEAD_DIM:]
        o_out[t["out_rows"], :] = t["acc"][:, :HEAD_DIM] / den
        l_out[t["out_rows"], :] = t["m"] + jnp.log2(den)

    for (window, d), o_out, l_out in zip(DILATED_PATTERNS, (o1, o2, o3), (l1, l2, l3)):
        assert window == BLK * d and (d == 1 or d % NC == 0)
        assert BLK * d <= (ATTN_PAD_N if d == 1 else ATTN_PAD_M * NC) and per_class % (BLK * max(d // NC, 1)) == 0
        n_units = seq // BLK
        assert n_units % ATTN_UNITS == 0

        def units(it, carry, d=d, o_out=o_out, l_out=l_out):
            ts = [qk(d, it * ATTN_UNITS + j) for j in range(ATTN_UNITS)]
            for t in ts:
                pv(t)
            for t in ts:
                store(t, o_out, l_out)
            return carry

        lax.fori_loop(0, n_units // ATTN_UNITS, units, 0)

    def combine(i, carry):
        rows = pl.ds(pl.multiple_of(i * ROWS, ROWS), ROWS)
        a1, a2, a3 = l1[rows, :], l2[rows, :], l3[rows, :]
        m = jnp.maximum(jnp.maximum(a1, a2), a3)
        e1, e2, e3 = jnp.exp2(a1 - m), jnp.exp2(a2 - m), jnp.exp2(a3 - m)
        out = (e1 * o1[rows, :] + e2 * o2[rows, :] + e3 * o3[rows, :]) / (e1 + e2 + e3)
        o_ref[rows, :] = out.astype(o_ref.dtype)
        return carry

    lax.fori_loop(0, seq // ROWS, combine, 0)


def _rope_tables(seq):
    half = HEAD_DIM // 2
    inv_freq = ROPE_THETA ** (-jnp.arange(half, dtype=F32) / half)
    ang = jnp.arange(seq, dtype=F32)[:, None] * inv_freq[None, :]
    cos, sin = jnp.cos(ang), jnp.sin(ang)
    return jnp.concatenate([cos, cos], axis=-1), jnp.concatenate([-sin, sin], axis=-1)


def dilated_attention(proj, *, batch, seq, col0):
    T = proj.shape[0]
    H = N_ATTN_HEADS
    c0 = col0 // HEAD_DIM
    cos, sin = _rope_tables(seq)

    def sec(k):
        return pl.BlockSpec((seq, HEAD_DIM), lambda b, h, k=k: (b, c0 + k * H + h))

    table = pl.BlockSpec((seq, HEAD_DIM), lambda b, h: (0, 0), pipeline_mode=pl.Buffered(1))
    slab = pltpu.VMEM((seq, HEAD_DIM), F32)
    padded_n = pltpu.VMEM((ATTN_PAD_N + seq, HEAD_DIM), F32)
    padded_m = pltpu.VMEM((ATTN_CLASSES * ATTN_PAD_M + seq, HEAD_DIM), F32)
    return pl.pallas_call(
        functools.partial(_attn_kernel, seq=seq),
        grid=(batch, H),
        in_specs=[sec(0), sec(1), sec(2), table, table],
        out_specs=pl.BlockSpec((seq, HEAD_DIM), lambda b, h: (b, h)),
        out_shape=jax.ShapeDtypeStruct((T, ATTN_WIDTH), BF16),
        scratch_shapes=[slab, padded_n, padded_n, slab, padded_m, padded_m] + [slab] * 6
                       + [pltpu.VMEM((2, HEAD_DIM, 2 * HEAD_DIM), F32)],
        compiler_params=_params(("parallel", "parallel")),
        name="dilated_attn",
    )(proj, proj, proj, cos, sin)


def _out_proj_kernel(*refs, route):
    if route:
        oh_ref, oa_ref, x_ref, w_ref, g_ref, wr_ref, xo_ref, h_ref, gate_ref, idx_ref = refs
    else:
        oh_ref, oa_ref, x_ref, w_ref, g_ref, xo_ref, h_ref = refs
    hw = oh_ref.shape[1]
    tm = x_ref.shape[0]
    sub = min(tm, OUT_PROJ_SUB_ROWS)
    for r0 in range(0, tm, sub):
        rows = slice(r0, r0 + sub)
        y = (jnp.dot(oh_ref[rows, :], w_ref[:hw, :], preferred_element_type=F32)
             + jnp.dot(oa_ref[rows, :], w_ref[hw:, :], preferred_element_type=F32))
        x = x_ref[rows, :] + y
        xo_ref[rows, :] = x
        h = _rms(x, g_ref[...])
        h_ref[rows, :] = h.astype(h_ref.dtype)
        if route:
            h_hi = h.astype(BF16)
            h_lo = (h - h_hi.astype(F32)).astype(BF16)
            logits = (jnp.dot(h_hi, wr_ref[0], preferred_element_type=F32)
                      + jnp.dot(h_lo, wr_ref[0], preferred_element_type=F32)
                      + jnp.dot(h_hi, wr_ref[1], preferred_element_type=F32))
            lane = lax.broadcasted_iota(jnp.int32, logits.shape, 1)
            lg = jnp.where(lane < N_EXPERTS, logits, -jnp.inf)
            m1 = jnp.max(lg, axis=-1, keepdims=True)
            i1 = jnp.min(jnp.where(lg == m1, lane, HEAD_DIM), axis=-1, keepdims=True)
            lg = jnp.where(lane == i1, -jnp.inf, lg)
            m2 = jnp.max(lg, axis=-1, keepdims=True)
            i2 = jnp.min(jnp.where(lg == m2, lane, HEAD_DIM), axis=-1, keepdims=True)
            e = jnp.exp(m2 - m1)
            gate_ref[rows, :] = jnp.where(lane == 0, 1.0 / (1.0 + e),
                                          jnp.where(lane == 1, e / (1.0 + e), 0.0))
            idx_ref[rows, :] = jnp.where(lane == 0, i1, jnp.where(lane == 1, i2, 0))


def out_proj_norm(o_hgrn, o_attn, x, w, gain, w_router=None, *, tm=512):
    T, D = x.shape
    tm = min(tm, T)
    route = w_router is not None
    row = lambda width: pl.BlockSpec((tm, width), lambda i: (i, 0))
    whole = lambda a: pl.BlockSpec(a.shape, lambda i: (0, 0), pipeline_mode=pl.Buffered(1))
    gain2 = gain.reshape(1, D).astype(F32)
    args = [o_hgrn, o_attn, x, w, gain2]
    in_specs = [row(o_hgrn.shape[1]), row(o_attn.shape[1]), row(D), whole(w), whole(gain2)]
    out_specs = [row(D), row(D)]
    out_shape = [jax.ShapeDtypeStruct((T, D), F32), jax.ShapeDtypeStruct((T, D), F32 if route else BF16)]
    if route:
        wr = jnp.zeros((D, HEAD_DIM), F32).at[:, :N_EXPERTS].set(w_router.astype(F32))
        wr_hi = wr.astype(BF16)
        wr = jnp.stack([wr_hi, (wr - wr_hi.astype(F32)).astype(BF16)])
        args.append(wr)
        in_specs.append(pl.BlockSpec(wr.shape, lambda i: (0, 0, 0), pipeline_mode=pl.Buffered(1)))
        out_specs += [row(HEAD_DIM), row(HEAD_DIM)]
        out_shape += [jax.ShapeDtypeStruct((T, HEAD_DIM), F32),
                      jax.ShapeDtypeStruct((T, HEAD_DIM), jnp.int32)]
    return pl.pallas_call(
        functools.partial(_out_proj_kernel, route=route),
        grid=(T // tm,),
        in_specs=in_specs,
        out_specs=out_specs,
        out_shape=out_shape,
        compiler_params=_params(("parallel",)),
        name="out_proj_route" if route else "out_proj",
    )(*args)


def _fresh_weights(be_ref, i):
    return (i == 0) | (be_ref[i] != be_ref[jnp.maximum(i - 1, 0)])


def _ffn_up_kernel(be_ref, na_ref, x_ref, wg_ref, wu_ref, a_ref, wg16, wu16):
    i = pl.program_id(1)
    active = i < na_ref[0]

    @pl.when(active & _fresh_weights(be_ref, i))
    def _():
        wg16[...] = wg_ref[...].astype(BF16)
        wu16[...] = wu_ref[...].astype(BF16)

    @pl.when(active)
    def _():
        x = x_ref[...]
        g = jnp.dot(x, wg16[...], preferred_element_type=F32)
        u = jnp.dot(x, wu16[...], preferred_element_type=F32)
        a_ref[...] = (_silu(g) * u).astype(a_ref.dtype)

    @pl.when(jnp.logical_not(active))
    def _():
        a_ref[...] = jnp.zeros_like(a_ref)


def _ffn_down_kernel(*refs, residual):
    if residual:
        be_ref, na_ref, a_ref, wd_ref, x_ref, o_ref, wd16 = refs
    else:
        be_ref, na_ref, a_ref, wd_ref, o_ref, wd16 = refs
    i = pl.program_id(1)
    active = i < na_ref[0]

    @pl.when(active & _fresh_weights(be_ref, i))
    def _():
        wd16[...] = wd_ref[...].astype(BF16)

    @pl.when(active)
    def _():
        y = jnp.dot(a_ref[...], wd16[...], preferred_element_type=F32)
        o_ref[...] = x_ref[...] + y if residual else y

    @pl.when(jnp.logical_not(active))
    def _():
        o_ref[...] = jnp.zeros_like(o_ref)


def _block(i, na):
    return jnp.minimum(i, na[0] - 1)


def ffn_up(rows, block_expert, n_active, w_gate, w_up, *, tm, tf):
    R, D = rows.shape
    F = w_gate.shape[2]
    tf = min(tf, F)
    wspec =pl.BlockSpec((None, D, tf), lambda f, i, be, na: (be[_block(i, na)], 0, f))
    return pl.pallas_call(
        _ffn_up_kernel,
        grid_spec=pltpu.PrefetchScalarGridSpec(
            num_scalar_prefetch=2,
            grid=(F // tf, R // tm),
            in_specs=[pl.BlockSpec((tm, D), lambda f, i, be, na: (_block(i, na), 0)), wspec, wspec],
            out_specs=pl.BlockSpec((tm, tf), lambda f, i, be, na: (i, f)),
            scratch_shapes=[pltpu.VMEM((D, tf), BF16), pltpu.VMEM((D, tf), BF16)]),
        out_shape=jax.ShapeDtypeStruct((R, F), BF16),
        compiler_params=_params(("arbitrary", "arbitrary")),
        name="ffn_up",
    )(block_expert, n_active, rows, w_gate, w_up)


def ffn_down(act, block_expert, n_active, w_down, x=None, *, tm, tn):
    R, F = act.shape
    D = w_down.shape[2]
    tn = min(tn, D)
    residual = x is not None
    tile = pl.BlockSpec((tm, tn), lambda n, i, be, na: (i, n))
    in_specs = [pl.BlockSpec((tm, F), lambda n, i, be, na: (_block(i, na), 0)),
                pl.BlockSpec((None, F, tn), lambda n, i, be, na: (be[_block(i, na)], 0, n))]
    args = [act, w_down]
    if residual:
        in_specs.append(tile)
        args.append(x)
    return pl.pallas_call(
        functools.partial(_ffn_down_kernel, residual=residual),
        grid_spec=pltpu.PrefetchScalarGridSpec(
            num_scalar_prefetch=2,
            grid=(D // tn, R // tm),
            in_specs=in_specs,
            out_specs=tile,
            scratch_shapes=[pltpu.VMEM((F, tn), BF16)]),
        out_shape=jax.ShapeDtypeStruct((R, D), F32),
        compiler_params=_params(("arbitrary", "arbitrary")),
        name="ffn_down",
    )(block_expert, n_active, *args)


def dense_ffn(h, x, w_gate, w_up, w_down, index, *, tm=1024, tf=512, tm_down=512, tn=512):
    T = x.shape[0]

    def blocks(t):
        return jnp.full((T // t,), index, jnp.int32), jnp.full((1,), T // t, jnp.int32)

    act = ffn_up(h, *blocks(tm), w_gate, w_up, tm=tm, tf=tf)
    return ffn_down(act, *blocks(tm_down), w_down, x, tm=tm_down, tn=tn)


def _row_copy(src_hbm, dst_ref, sem, src_row, dst_row):
    return pltpu.make_async_copy(src_hbm.at[pl.ds(src_row, 1), :], dst_ref.at[pl.ds(dst_row, 1), :], sem)


def _gather_rows_kernel(na_ref, idx_ref, nxt_ref, h_hbm, o_ref, buf, sem):
    tm = buf.shape[1]
    i = pl.program_id(0)
    slot = i % 2

    def rows(ids_ref, s, start):
        def body(g, c):
            for j in range(DMA_UNROLL):
                r = g * DMA_UNROLL + j
                cp = _row_copy(h_hbm, buf.at[s], sem.at[s], ids_ref[0, 0, r], r)
                if start:
                    cp.start(priority=j % 2)
                else:
                    cp.wait()
            return c

        lax.fori_loop(0, tm // DMA_UNROLL, body, 0)

    @pl.when(i == 0)
    def _():
        rows(idx_ref, 0, True)

    @pl.when(i + 1 < na_ref[0])
    def _():
        rows(nxt_ref, 1 - slot, True)

    @pl.when(i < na_ref[0])
    def _():
        rows(idx_ref, slot, False)
        o_ref[...] = buf[slot].astype(o_ref.dtype)

    @pl.when(i >= na_ref[0])
    def _():
        o_ref[...] = jnp.zeros_like(o_ref)


def gather_rows(h, row_token, n_active, *, tm):
    T, D = h.shape
    R = row_token.shape[0]
    ids = row_token.reshape(R // tm, 1, tm)
    return pl.pallas_call(
        _gather_rows_kernel,
        grid_spec=pltpu.PrefetchScalarGridSpec(
            num_scalar_prefetch=1,
            grid=(R // tm,),
            in_specs=[pl.BlockSpec((1, 1, tm), lambda i, na: (_block(i, na), 0, 0),
                                   memory_space=pltpu.SMEM),
                      pl.BlockSpec((1, 1, tm), lambda i, na: (_block(i + 1, na), 0, 0),
                                   memory_space=pltpu.SMEM),
                      pl.BlockSpec(memory_space=pl.ANY)],
            out_specs=pl.BlockSpec((tm, D), lambda i, na: (i, 0)),
            scratch_shapes=[pltpu.VMEM((2, tm, D), F32), pltpu.SemaphoreType.DMA((2,))]),
        out_shape=jax.ShapeDtypeStruct((R, D), BF16),
        compiler_params=_params(("arbitrary",)),
        name="gather_rows",
    )(n_active, ids, ids, h)


def _combine_kernel(*refs, final):
    if final:
        dest_ref, x_ref, gate_ref, gain_ref, y_hbm, o_ref, buf0, buf1, sem = refs
    else:
        dest_ref, x_ref, gate_ref, y_hbm, o_ref, buf0, buf1, sem = refs
    tm = buf0.shape[0]

    def copies(r):
        return (_row_copy(y_hbm, buf0, sem.at[0], dest_ref[0, 0, 2 * r], r),
                _row_copy(y_hbm, buf1, sem.at[1], dest_ref[0, 0, 2 * r + 1], r))

    def start(g, c):
        for j in range(DMA_UNROLL):
            first, second = copies(g * DMA_UNROLL + j)
            first.start(priority=0)
            second.start(priority=1)
        return c

    def wait(g, c):
        for j in range(DMA_UNROLL):
            for cp in copies(g * DMA_UNROLL + j):
                cp.wait()
        return c

    lax.fori_loop(0, tm // DMA_UNROLL, start, 0)
    lax.fori_loop(0, tm // DMA_UNROLL, wait, 0)
    gates = gate_ref[...]
    x = x_ref[...] + gates[:, 0:1] * buf0[...] + gates[:, 1:2] * buf1[...]
    o_ref[...] = _rms(x, gain_ref[...]) if final else x


def combine_rows(x, y_rows, dest, gates, final_gain=None, *, tm=256):
    T, D = x.shape
    final = final_gain is not None
    row = pl.BlockSpec((tm, D), lambda i: (i, 0))
    in_specs = [pl.BlockSpec((1, 1, 2 * tm), lambda i: (i, 0, 0), memory_space=pltpu.SMEM),
                row, pl.BlockSpec((tm, gates.shape[1]), lambda i: (i, 0))]
    args = [dest.reshape(T // tm, 1, 2 * tm), x, gates]
    if final:
        in_specs.append(pl.BlockSpec((1, D), lambda i: (0, 0)))
        args.append(final_gain.reshape(1, D).astype(F32))
    in_specs.append(pl.BlockSpec(memory_space=pl.ANY))
    args.append(y_rows)
    return pl.pallas_call(
        functools.partial(_combine_kernel, final=final),
        grid=(T // tm,),
        in_specs=in_specs,
        out_specs=row,
        out_shape=jax.ShapeDtypeStruct((T, D), F32),
        scratch_shapes=[pltpu.VMEM((tm, D), F32), pltpu.VMEM((tm, D), F32),
                        pltpu.SemaphoreType.DMA((2,))],
        compiler_params=_params(("arbitrary",)),
        name="combine_rows",
    )(*args)


def _rms_rows_kernel(x_ref, g_ref, o_ref):
    o_ref[...] = _rms(x_ref[...], g_ref[...])


def rms_rows(x, gain, *, tm=512):
    T, D = x.shape
    tm = min(tm, T)
    return pl.pallas_call(
        _rms_rows_kernel,
        grid=(T // tm,),
        in_specs=[pl.BlockSpec((tm, D), lambda i: (i, 0)), pl.BlockSpec((1, D), lambda i: (0, 0))],
        out_specs=pl.BlockSpec((tm, D), lambda i: (i, 0)),
        out_shape=jax.ShapeDtypeStruct((T, D), F32),
        compiler_params=_params(("parallel",)),
        name="final_norm",
    )(x, gain.reshape(1, D).astype(F32))


def moe_layer(h, x, gates, top_idx, w_gate, w_up, w_down, index, final_gain=None, *,
              tm=512, tm_down=512):
    T, D = x.shape
    E = w_gate.shape[1]
    w_gate, w_up, w_down = (w.reshape((-1,) + w.shape[2:]) for w in (w_gate, w_up, w_down))
    n_assign = 2 * T
    flat_expert = top_idx[:, :2].reshape(n_assign)
    onehot = (flat_expert[:, None] == jnp.arange(E, dtype=jnp.int32)[None, :]).astype(jnp.int32)
    csum = jnp.cumsum(onehot, axis=0)
    rank = jnp.take_along_axis(csum, flat_expert[:, None], axis=1)[:, 0] - 1
    counts = csum[-1]
    padded = (counts + tm - 1) // tm * tm
    padded_ends = jnp.cumsum(padded)
    dest = (padded_ends - padded)[flat_expert] + rank
    n_blocks = -(-n_assign // tm) + E
    row_token = jnp.zeros((n_blocks * tm,), jnp.int32).at[dest].set(
        jnp.arange(n_assign, dtype=jnp.int32) // 2)
    block_expert = jnp.minimum(
        jnp.searchsorted(padded_ends, jnp.arange(n_blocks, dtype=jnp.int32) * tm, side='right'),
        E - 1).astype(jnp.int32) + index * E
    n_active = (padded_ends[-1:] // tm).astype(jnp.int32)
    rows = gather_rows(h, row_token, n_active, tm=tm)
    act = ffn_up(rows, block_expert, n_active, w_gate, w_up, tm=tm, tf=1024)
    sub = tm // tm_down
    y_rows = ffn_down(act, jnp.repeat(block_expert, sub), n_active * sub, w_down, tm=tm_down, tn=512)
    return combine_rows(x, y_rows, dest.astype(jnp.int32), gates, final_gain)


def kernel(x, norm_mix, w_in, lb_logits, hgrn_norm, w_out, norm_ffn, dense_w_gate, dense_w_up,
           dense_w_down, moe_router, moe_w_gate, moe_w_up, moe_w_down, final_norm):
    B, S, D = x.shape
    depth = w_in.shape[0]
    lb_sm = jax.nn.softmax(lb_logits.astype(F32), axis=0)
    lower_bounds = jnp.cumsum(lb_sm, axis=0) - lb_sm[0:1]
    xt = x.reshape(B * S, D)
    for layer in range(depth):
        proj = norm_matmul(xt, norm_mix[layer], w_in, layer)
        o_hgrn = hgrn2(proj, lower_bounds[layer], hgrn_norm[layer], batch=B, seq=S)
        o_attn = dilated_attention(proj, batch=B, seq=S, col0=4 * HGRN_WIDTH)
        i = layer // 2
        last = layer == depth - 1
        if layer % 2 == 0:
            xt, h = out_proj_norm(o_hgrn, o_attn, xt, w_out[layer].astype(BF16), norm_ffn[layer])
            xt = dense_ffn(h, xt, dense_w_gate, dense_w_up, dense_w_down, i)
            if last:
                xt = rms_rows(xt, final_norm)
        else:
            xt, h, gates, idx = out_proj_norm(o_hgrn, o_attn, xt, w_out[layer].astype(BF16),
                                              norm_ffn[layer], moe_router[i])
            xt = moe_layer(h, xt, gates, idx, moe_w_gate, moe_w_up, moe_w_down, i,
                           final_norm if last else None)
    return xt.reshape(B, S, D)
```

```python
import functools
import math

import jax
import jax.numpy as jnp
from jax import lax
from jax.experimental import pallas as pl
from jax.experimental.pallas import tpu as pltpu

F32 = jnp.float32
BF16 = jnp.bfloat16

HEAD_DIM = 128
N_HGRN_HEADS = 8
N_ATTN_HEADS = 8
HGRN_WIDTH = N_HGRN_HEADS * HEAD_DIM
ATTN_WIDTH = N_ATTN_HEADS * HEAD_DIM
HGRN_CHUNK = 64
HGRN_SUB = 8
HGRN_HEADS_PER_STEP = 8
DILATED_PATTERNS = ((128, 1), (512, 4), (2048, 16))
ATTN_CLASSES = 4
ATTN_PAD_N = 128
ATTN_PAD_M = 512
ATTN_UNITS = 16
SUBLANES = 8
OUT_PROJ_SUB_ROWS = 128
ROPE_THETA = 10000.0
N_EXPERTS = 8
NORM_EPS = 1e-6
VMEM_LIMIT = 60 * 1024 * 1024

LAYER_TILING = (
    dict(in_proj=dict(), hgrn=dict(), attn=dict(), ffn=dict()),
    dict(in_proj=dict(tn=512), hgrn=dict(tt=1024), attn=dict(n_par=8), ffn=dict()),
    dict(in_proj=dict(), hgrn=dict(tt=256), attn=dict(n_par=32), ffn=dict(tm=2048)),
    dict(in_proj=dict(), hgrn=dict(heads=4), attn=dict(), ffn=dict()),
)

_NT = (((1,), (1,)), ((), ()))
_TN = (((0,), (0,)), ((), ()))


def _params(semantics):
    return pltpu.CompilerParams(dimension_semantics=semantics, vmem_limit_bytes=VMEM_LIMIT)


def _silu(z):
    return z * jax.nn.sigmoid(z)


def _rms(z, gain):
    ms = jnp.mean(z * z, axis=-1, keepdims=True)
    return z * lax.rsqrt(ms + NORM_EPS) * gain


def _in_proj_kernel(x_ref, g_ref, w_ref, o_ref, h_ref):
    @pl.when(pl.program_id(1) == 0)
    def _():
        h_ref[...] = _rms(x_ref[...], g_ref[...]).astype(BF16)

    o_ref[...] = jnp.dot(h_ref[...], w_ref[...].astype(BF16),
                         preferred_element_type=F32).astype(o_ref.dtype)


def norm_matmul(x, gain, w, index, *, tm=1024, tn=1024):
    T, D = x.shape
    N = w.shape[2]
    tm, tn = min(tm, T), min(tn, N)
    return pl.pallas_call(
        _in_proj_kernel,
        grid=(T // tm, N // tn),
        in_specs=[pl.BlockSpec((tm, D), lambda i, j: (i, 0)),
                  pl.BlockSpec((1, D), lambda i, j: (0, 0)),
                  pl.BlockSpec((None, D, tn), lambda i, j: (index, 0, j))],
        out_specs=pl.BlockSpec((tm, tn), lambda i, j: (i, j)),
        out_shape=jax.ShapeDtypeStruct((T, N), BF16),
        scratch_shapes=[pltpu.VMEM((tm, D), BF16)],
        compiler_params=_params(("parallel", "arbitrary")),
        name="in_proj",
    )(x, gain.reshape(1, D).astype(F32), w)


def _hgrn_kernel(q_ref, f_ref, i_ref, g_ref, lb_ref, gain_ref, o_ref,
                 st_ref, k_ref, tril_ref, select_ref, causal_ref, *, n_chunks, heads):
    C, SUB = HGRN_CHUNK, HGRN_SUB

    @pl.when(pl.program_id(2) == 0)
    def _():
        st_ref[...] = jnp.zeros_like(st_ref)
        row = lax.broadcasted_iota(jnp.int32, (C, C), 0)
        col = lax.broadcasted_iota(jnp.int32, (C, C), 1)
        tril_ref[...] = (col <= row).astype(BF16)
        group = lax.shift_right_logical(
            lax.broadcasted_iota(jnp.int32, (SUB * HEAD_DIM, HEAD_DIM), 0), HEAD_DIM.bit_length() - 1)
        select_ref[...] = (group == lax.broadcasted_iota(jnp.int32, (SUB * HEAD_DIM, HEAD_DIM), 1)
                           ).astype(BF16)
        sub_row = lax.broadcasted_iota(jnp.int32, (SUB, SUB, HEAD_DIM), 1)
        key = lax.broadcasted_iota(jnp.int32, (SUB, SUB, HEAD_DIM), 0)
        causal_ref[...] = jnp.where(sub_row >= key, 0.0, -jnp.inf)

    pairs = []
    span = C
    while span > SUB:
        half = span // 2
        pairs += [(base + half, base + span, base, base + half) for base in range(0, C, span)]
        span = half

    def gates(hh, rows):
        lanes = slice(hh * HEAD_DIM, (hh + 1) * HEAD_DIM)
        lb = lb_ref[:, lanes]
        f = lb + (1.0 - lb) * jax.nn.sigmoid(f_ref[rows, lanes].astype(F32))
        g = jnp.log2(f)
        g_hi = g.astype(BF16)
        g_lo = (g - g_hi.astype(F32)).astype(BF16)
        tril = tril_ref[...]
        b = (jnp.dot(tril, g_hi, preferred_element_type=F32)
             + jnp.dot(tril, g_lo, preferred_element_type=F32))
        return dict(lanes=lanes, b=b, kk=1.0 - f, q=_silu(q_ref[rows, lanes].astype(F32)),
                    v=i_ref[rows, lanes].astype(F32))

    def scores(hh, s):
        b, kk, q, v = s["b"], s["kk"], s["q"], s["v"]
        k_scr = k_ref.at[hh]
        k_scr[...] = jnp.log2(kk) - b
        s["v16"] = v.astype(BF16)
        b_end = b[C - 1:C, :]
        st = st_ref[hh]
        s["o"] = lax.dot_general((q * jnp.exp2(b)).astype(BF16), st.astype(BF16), _NT,
                                 preferred_element_type=F32)
        kd = (kk * jnp.exp2(b_end - b)).astype(BF16)
        st_ref[hh] = st * jnp.exp2(b_end) + lax.dot_general(s["v16"], kd, _TN,
                                                            preferred_element_type=F32)
        s["a_off"] = []
        for t0, t1, s0, s1 in pairs:
            b_mid = b[s1 - 1:s1, :]
            qa = (q[t0:t1] * jnp.exp2(b[t0:t1] - b_mid)).astype(BF16)
            ka = (kk[s0:s1] * jnp.exp2(b_mid - b[s0:s1])).astype(BF16)
            s["a_off"].append(lax.dot_general(qa, ka, _NT, preferred_element_type=F32))
        blocks = []
        for r in range(0, C, SUB):
            qs = q[r:r + SUB]
            bs = b[r:r + SUB]
            groups = []
            for j in range(SUB):
                d = bs + jnp.broadcast_to(k_scr[r + j:r + j + 1, :], (SUB, HEAD_DIM))
                if j:
                    d = d + causal_ref[j]
                groups.append(qs * jnp.exp2(d))
            blocks.append(jnp.concatenate(groups, axis=1))
        a = jnp.dot(jnp.concatenate(blocks, axis=0).astype(BF16), select_ref[...],
                    preferred_element_type=F32)
        s["a_diag"] = jnp.concatenate(
            [a[r:r + SUB] if r == 0 else pltpu.roll(a[r:r + SUB], r, 1) for r in range(0, C, SUB)], axis=0)

    def values(hh, s):
        parts = [jnp.zeros((SUB, HEAD_DIM), F32)] + [None] * (C // SUB - 1)

        def add(idx, piece):
            parts[idx] = piece if parts[idx] is None else parts[idx] + piece

        for (t0, t1, s0, s1), a in zip(pairs, s["a_off"]):
            blk = jnp.dot(a.astype(BF16), s["v16"][s0:s1], preferred_element_type=F32)
            for j in range((t1 - t0) // SUB):
                add(t0 // SUB + j, blk[j * SUB:(j + 1) * SUB])
        v_rows = jnp.concatenate([s["v16"], jnp.zeros((HEAD_DIM - C, HEAD_DIM), BF16)], axis=0)
        diag = jnp.dot(s["a_diag"].astype(BF16), v_rows, preferred_element_type=F32)
        s["o"] = s["o"] + diag + jnp.concatenate(parts, axis=0)

    def finish(s, rows):
        lanes = s["lanes"]
        o = _rms(s["o"], gain_ref[:, lanes]) * _silu(g_ref[rows, lanes].astype(F32))
        o_ref[rows, lanes] = o.astype(o_ref.dtype)

    def chunk(ci, carry):
        rows = pl.ds(pl.multiple_of(ci * C, C), C)
        state = [gates(hh, rows) for hh in range(heads)]
        for hh in range(heads):
            scores(hh, state[hh])
        for hh in range(heads):
            values(hh, state[hh])
        for hh in range(heads):
            finish(state[hh], rows)
        return carry

    lax.fori_loop(0, n_chunks, chunk, 0)


def hgrn2(proj, lower_bound, out_gain, *, batch, seq, tt=512, heads=HGRN_HEADS_PER_STEP):
    T = proj.shape[0]
    H = N_HGRN_HEADS // heads
    width = heads * HEAD_DIM
    tt = min(tt, seq)
    nt = seq // tt

    def sec(k):
        return pl.BlockSpec((tt, width), lambda b, h, c, k=k: (b * nt + c, k * H + h))

    vec = pl.BlockSpec((1, width), lambda b, h, c: (0, h))
    per_head = pltpu.VMEM((heads, HGRN_CHUNK, HEAD_DIM), F32)
    return pl.pallas_call(
        functools.partial(_hgrn_kernel, n_chunks=tt // HGRN_CHUNK, heads=heads),
        grid=(batch, H, nt),
        in_specs=[sec(0), sec(1), sec(2), sec(3), vec, vec],
        out_specs=pl.BlockSpec((tt, width), lambda b, h, c: (b * nt + c, h)),
        out_shape=jax.ShapeDtypeStruct((T, HGRN_WIDTH), BF16),
        scratch_shapes=[pltpu.VMEM((heads, HEAD_DIM, HEAD_DIM), F32), per_head,
                        pltpu.VMEM((HGRN_CHUNK, HGRN_CHUNK), BF16),
                        pltpu.VMEM((HGRN_SUB * HEAD_DIM, HEAD_DIM), BF16),
                        pltpu.VMEM((HGRN_SUB, HGRN_SUB, HEAD_DIM), F32)],
        compiler_params=_params(("parallel", "parallel", "arbitrary")),
        name="hgrn2",
    )(proj, proj, proj, proj,
      lower_bound.reshape(1, HGRN_WIDTH).astype(F32), out_gain.reshape(1, HGRN_WIDTH).astype(F32))


def _attn_kernel(q_ref, k_ref, v_ref, cos_ref, sin_ref, o_ref,
                 qn, kn, vn, qm, km, vm, o1, o2, o3, l1, l2, l3, bias_ref, *, seq, n_par):
    BLK = 128
    ROWS = 512
    NC = ATTN_CLASSES
    per_class = seq // NC
    scale = HEAD_DIM ** -0.5 * math.log2(math.e)

    def rope(i, carry):
        rows = pl.ds(pl.multiple_of(i * ROWS, ROWS), ROWS)
        c = cos_ref[rows, :]
        s = sin_ref[rows, :]
        q = q_ref[rows, :].astype(F32)
        k = k_ref[rows, :].astype(F32)
        qn[rows, :] = (q * c + pltpu.roll(q, HEAD_DIM // 2, 1) * s) * scale
        padded = pl.ds(pl.multiple_of(i * ROWS, ROWS) + ATTN_PAD_N, ROWS)
        kn[padded, :] = k * c + pltpu.roll(k, HEAD_DIM // 2, 1) * s
        vn[padded, :] = v_ref[rows, :].astype(F32)
        return carry

    lax.fori_loop(0, seq // ROWS, rope, 0)

    kn[0:ATTN_PAD_N, :] = jnp.zeros((ATTN_PAD_N, HEAD_DIM), F32)
    vn[0:ATTN_PAD_N, :] = jnp.zeros((ATTN_PAD_N, HEAD_DIM), F32)
    for c in range(NC):
        base = c * (ATTN_PAD_M + per_class)
        km[base:base + ATTN_PAD_M, :] = jnp.zeros((ATTN_PAD_M, HEAD_DIM), F32)
        vm[base:base + ATTN_PAD_M, :] = jnp.zeros((ATTN_PAD_M, HEAD_DIM), F32)

    def regroup(i, carry):
        j0 = pl.multiple_of(i * ROWS, ROWS)
        for c in range(NC):
            src = pl.ds(NC * j0 + c, ROWS, stride=NC)
            qm[pl.ds(c * per_class + j0, ROWS), :] = qn[src, :]
            dst = pl.ds(c * (ATTN_PAD_M + per_class) + ATTN_PAD_M + j0, ROWS)
            km[dst, :] = kn[pl.ds(NC * j0 + c + ATTN_PAD_N, ROWS, stride=NC), :]
            vm[dst, :] = vn[pl.ds(NC * j0 + c + ATTN_PAD_N, ROWS, stride=NC), :]
        return carry

    lax.fori_loop(0, per_class // ROWS, regroup, 0)

    qi = lax.broadcasted_iota(jnp.int32, (BLK, 2 * BLK), 0)
    kj = lax.broadcasted_iota(jnp.int32, (BLK, 2 * BLK), 1)
    band = (kj >= qi) & (kj <= qi + BLK)
    bias_ref[0] = jnp.where(band & (kj >= BLK), 0.0, -jnp.inf)
    bias_ref[1] = jnp.where(band, 0.0, -jnp.inf)
    ones = jnp.ones((2 * BLK, HEAD_DIM), BF16)

    def ds(start, size, stride):
        return pl.ds(start, size) if stride == 1 else pl.ds(start, size, stride=stride)

    def unit_rows(d, u):
        if d == 1:
            return u, pl.ds(u * BLK, BLK), pl.ds(u * BLK + (ATTN_PAD_N - BLK), 2 * BLK), pl.ds(u * BLK, BLK)
        step = d // NC
        nb = per_class // (BLK * step)
        n = u & (nb - 1)
        r = lax.shift_right_logical(u, nb.bit_length() - 1) & (step - 1)
        c = lax.shift_right_logical(u, (nb * step).bit_length() - 1)
        j0 = n * (BLK * step) + r
        return (n, ds(c * per_class + j0, BLK, step),
                ds(c * (ATTN_PAD_M + per_class) + (ATTN_PAD_M - BLK * step) + j0, 2 * BLK, step),
                pl.ds(NC * j0 + c, BLK, stride=d))

    def qk(d, u):
        q_src, k_src, v_src = (qn, kn, vn) if d == 1 else (qm, km, vm)
        n, q_rows, k_rows, out_rows = unit_rows(d, u)
        q = q_src[q_rows, :].astype(BF16)
        k = k_src[k_rows, :].astype(BF16)
        v = v_src[k_rows, :].astype(BF16)
        s = lax.dot_general(q, k, _NT, preferred_element_type=F32)
        return dict(n=n, out_rows=out_rows, s=s, v=v)

    def pv(t):
        s = t["s"] + bias_ref[jnp.minimum(t["n"], 1)]
        t["m"] = jnp.max(s, axis=-1, keepdims=True)
        p = jnp.exp2(s - t["m"]).astype(BF16)
        t["acc"] = jnp.dot(p, jnp.concatenate([t["v"], ones], axis=1), preferred_element_type=F32)

    def store(t, o_out, l_out):
        den = t["acc"][:, HEAD_DIM:]
        o_out[t["out_rows"], :] = t["acc"][:, :HEAD_DIM] / den
        l_out[t["out_rows"], :] = t["m"] + jnp.log2(den)

    for (window, d), o_out, l_out in zip(DILATED_PATTERNS, (o1, o2, o3), (l1, l2, l3)):
        assert window == BLK * d and (d == 1 or d % NC == 0)
        assert BLK * d <= (ATTN_PAD_N if d == 1 else ATTN_PAD_M * NC) and per_class % (BLK * max(d // NC, 1)) == 0
        n_units = seq // BLK
        assert n_units % n_par == 0

        def units(it, carry, d=d, o_out=o_out, l_out=l_out):
            ts = [qk(d, it * n_par + j) for j in range(n_par)]
            for t in ts:
                pv(t)
            for t in ts:
                store(t, o_out, l_out)
            return carry

        lax.fori_loop(0, n_units // n_par, units, 0)

    def combine(i, carry):
        rows = pl.ds(pl.multiple_of(i * ROWS, ROWS), ROWS)
        a1, a2, a3 = l1[rows, :], l2[rows, :], l3[rows, :]
        m = jnp.maximum(jnp.maximum(a1, a2), a3)
        e1, e2, e3 = jnp.exp2(a1 - m), jnp.exp2(a2 - m), jnp.exp2(a3 - m)
        out = (e1 * o1[rows, :] + e2 * o2[rows, :] + e3 * o3[rows, :]) / (e1 + e2 + e3)
        o_ref[rows, :] = out.astype(o_ref.dtype)
        return carry

    lax.fori_loop(0, seq // ROWS, combine, 0)


def _rope_tables(seq):
    half = HEAD_DIM // 2
    inv_freq = ROPE_THETA ** (-jnp.arange(half, dtype=F32) / half)
    ang = jnp.arange(seq, dtype=F32)[:, None] * inv_freq[None, :]
    cos, sin = jnp.cos(ang), jnp.sin(ang)
    return jnp.concatenate([cos, cos], axis=-1), jnp.concatenate([-sin, sin], axis=-1)


def dilated_attention(proj, *, batch, seq, col0, n_par=ATTN_UNITS):
    T = proj.shape[0]
    H = N_ATTN_HEADS
    c0 = col0 // HEAD_DIM
    cos, sin = _rope_tables(seq)

    def sec(k):
        return pl.BlockSpec((seq, HEAD_DIM), lambda b, h, k=k: (b, c0 + k * H + h))

    table = pl.BlockSpec((seq, HEAD_DIM), lambda b, h: (0, 0), pipeline_mode=pl.Buffered(1))
    slab = pltpu.VMEM((seq, HEAD_DIM), F32)
    padded_n = pltpu.VMEM((ATTN_PAD_N + seq, HEAD_DIM), F32)
    padded_m = pltpu.VMEM((ATTN_CLASSES * ATTN_PAD_M + seq, HEAD_DIM), F32)
    return pl.pallas_call(
        functools.partial(_attn_kernel, seq=seq, n_par=n_par),
        grid=(batch, H),
        in_specs=[sec(0), sec(1), sec(2), table, table],
        out_specs=pl.BlockSpec((seq, HEAD_DIM), lambda b, h: (b, h)),
        out_shape=jax.ShapeDtypeStruct((T, ATTN_WIDTH), BF16),
        scratch_shapes=[slab, padded_n, padded_n, slab, padded_m, padded_m] + [slab] * 6
                       + [pltpu.VMEM((2, HEAD_DIM, 2 * HEAD_DIM), F32)],
        compiler_params=_params(("parallel", "parallel")),
        name="dilated_attn",
    )(proj, proj, proj, cos, sin)


def _out_proj_kernel(*refs, route):
    if route:
        oh_ref, oa_ref, x_ref, w_ref, g_ref, wr_ref, xo_ref, h_ref, gate_ref, idx_ref = refs
    else:
        oh_ref, oa_ref, x_ref, w_ref, g_ref, xo_ref, h_ref = refs
    hw = oh_ref.shape[1]
    tm = x_ref.shape[0]
    sub = min(tm, OUT_PROJ_SUB_ROWS)
    for r0 in range(0, tm, sub):
        rows = slice(r0, r0 + sub)
        y = (jnp.dot(oh_ref[rows, :], w_ref[:hw, :], preferred_element_type=F32)
             + jnp.dot(oa_ref[rows, :], w_ref[hw:, :], preferred_element_type=F32))
        x = x_ref[rows, :] + y
        xo_ref[rows, :] = x
        h = _rms(x, g_ref[...])
        h_ref[rows, :] = h.astype(h_ref.dtype)
        if route:
            h_hi = h.astype(BF16)
            h_lo = (h - h_hi.astype(F32)).astype(BF16)
            logits = (jnp.dot(h_hi, wr_ref[0], preferred_element_type=F32)
                      + jnp.dot(h_lo, wr_ref[0], preferred_element_type=F32)
                      + jnp.dot(h_hi, wr_ref[1], preferred_element_type=F32))
            lane = lax.broadcasted_iota(jnp.int32, logits.shape, 1)
            lg = jnp.where(lane < N_EXPERTS, logits, -jnp.inf)
            m1 = jnp.max(lg, axis=-1, keepdims=True)
            i1 = jnp.min(jnp.where(lg == m1, lane, HEAD_DIM), axis=-1, keepdims=True)
            lg = jnp.where(lane == i1, -jnp.inf, lg)
            m2 = jnp.max(lg, axis=-1, keepdims=True)
            i2 = jnp.min(jnp.where(lg == m2, lane, HEAD_DIM), axis=-1, keepdims=True)
            e = jnp.exp(m2 - m1)
            gate_ref[rows, :] = jnp.where(lane == 0, 1.0 / (1.0 + e),
                                          jnp.where(lane == 1, e / (1.0 + e), 0.0))
            idx_ref[rows, :] = jnp.where(lane == 0, i1, jnp.where(lane == 1, i2, 0))


def out_proj_norm(o_hgrn, o_attn, x, w, gain, w_router=None, *, tm=512):
    T, D = x.shape
    tm = min(tm, T)
    route = w_router is not None
    row = lambda width: pl.BlockSpec((tm, width), lambda i: (i, 0))
    whole = lambda a: pl.BlockSpec(a.shape, lambda i: (0, 0), pipeline_mode=pl.Buffered(1))
    gain2 = gain.reshape(1, D).astype(F32)
    args = [o_hgrn, o_attn, x, w, gain2]
    in_specs = [row(o_hgrn.shape[1]), row(o_attn.shape[1]), row(D), whole(w), whole(gain2)]
    out_specs = [row(D), row(D)]
    out_shape = [jax.ShapeDtypeStruct((T, D), F32), jax.ShapeDtypeStruct((T, D), F32 if route else BF16)]
    if route:
        wr = jnp.zeros((D, HEAD_DIM), F32).at[:, :N_EXPERTS].set(w_router.astype(F32))
        wr_hi = wr.astype(BF16)
        wr = jnp.stack([wr_hi, (wr - wr_hi.astype(F32)).astype(BF16)])
        args.append(wr)
        in_specs.append(pl.BlockSpec(wr.shape, lambda i: (0, 0, 0), pipeline_mode=pl.Buffered(1)))
        out_specs += [row(HEAD_DIM), row(HEAD_DIM)]
        out_shape += [jax.ShapeDtypeStruct((T, HEAD_DIM), F32),
                      jax.ShapeDtypeStruct((T, HEAD_DIM), jnp.int32)]
    return pl.pallas_call(
        functools.partial(_out_proj_kernel, route=route),
        grid=(T // tm,),
        in_specs=in_specs,
        out_specs=out_specs,
        out_shape=out_shape,
        compiler_params=_params(("parallel",)),
        name="out_proj_route" if route else "out_proj",
    )(*args)


def _fresh_weights(be_ref, i):
    return (i == 0) | (be_ref[i] != be_ref[jnp.maximum(i - 1, 0)])


def _ffn_up_kernel(be_ref, na_ref, x_ref, wg_ref, wu_ref, a_ref, wg16, wu16):
    i = pl.program_id(1)
    active = i < na_ref[0]

    @pl.when(active & _fresh_weights(be_ref, i))
    def _():
        wg16[...] = wg_ref[...].astype(BF16)
        wu16[...] = wu_ref[...].astype(BF16)

    @pl.when(active)
    def _():
        x = x_ref[...]
        g = jnp.dot(x, wg16[...], preferred_element_type=F32)
        u = jnp.dot(x, wu16[...], preferred_element_type=F32)
        a_ref[...] = (_silu(g) * u).astype(a_ref.dtype)

    @pl.when(jnp.logical_not(active))
    def _():
        a_ref[...] = jnp.zeros_like(a_ref)


def _ffn_down_kernel(*refs, residual):
    if residual:
        be_ref, na_ref, a_ref, wd_ref, x_ref, o_ref, wd16 = refs
    else:
        be_ref, na_ref, a_ref, wd_ref, o_ref, wd16 = refs
    i = pl.program_id(1)
    active = i < na_ref[0]

    @pl.when(active & _fresh_weights(be_ref, i))
    def _():
        wd16[...] = wd_ref[...].astype(BF16)

    @pl.when(active)
    def _():
        y = jnp.dot(a_ref[...], wd16[...], preferred_element_type=F32)
        o_ref[...] = x_ref[...] + y if residual else y

    @pl.when(jnp.logical_not(active))
    def _():
        o_ref[...] = jnp.zeros_like(o_ref)


def _block(i, na):
    return jnp.minimum(i, na[0] - 1)


def ffn_up(rows, block_expert, n_active, w_gate, w_up, *, tm, tf):
    R, D = rows.shape
    F = w_gate.shape[2]
    tf = min(tf, F)
    wspec =pl.BlockSpec((None, D, tf), lambda f, i, be, na: (be[_block(i, na)], 0, f))
    return pl.pallas_call(
        _ffn_up_kernel,
        grid_spec=pltpu.PrefetchScalarGridSpec(
            num_scalar_prefetch=2,
            grid=(F // tf, R // tm),
            in_specs=[pl.BlockSpec((tm, D), lambda f, i, be, na: (_block(i, na), 0)), wspec, wspec],
            out_specs=pl.BlockSpec((tm, tf), lambda f, i, be, na: (i, f)),
            scratch_shapes=[pltpu.VMEM((D, tf), BF16), pltpu.VMEM((D, tf), BF16)]),
        out_shape=jax.ShapeDtypeStruct((R, F), BF16),
        compiler_params=_params(("arbitrary", "arbitrary")),
        name="ffn_up",
    )(block_expert, n_active, rows, w_gate, w_up)


def ffn_down(act, block_expert, n_active, w_down, x=None, *, tm, tn):
    R, F = act.shape
    D = w_down.shape[2]
    tn = min(tn, D)
    residual = x is not None
    tile = pl.BlockSpec((tm, tn), lambda n, i, be, na: (i, n))
    in_specs = [pl.BlockSpec((tm, F), lambda n, i, be, na: (_block(i, na), 0)),
                pl.BlockSpec((None, F, tn), lambda n, i, be, na: (be[_block(i, na)], 0, n))]
    args = [act, w_down]
    if residual:
        in_specs.append(tile)
        args.append(x)
    return pl.pallas_call(
        functools.partial(_ffn_down_kernel, residual=residual),
        grid_spec=pltpu.PrefetchScalarGridSpec(
            num_scalar_prefetch=2,
            grid=(D // tn, R // tm),
            in_specs=in_specs,
            out_specs=tile,
            scratch_shapes=[pltpu.VMEM((F, tn), BF16)]),
        out_shape=jax.ShapeDtypeStruct((R, D), F32),
        compiler_params=_params(("arbitrary", "arbitrary")),
        name="ffn_down",
    )(block_expert, n_active, *args)


def dense_ffn(h, x, w_gate, w_up, w_down, index, *, tm=1024, tf=512, tm_down=512, tn=512):
    T = x.shape[0]

    def blocks(t):
        return jnp.full((T // t,), index, jnp.int32), jnp.full((1,), T // t, jnp.int32)

    act = ffn_up(h, *blocks(tm), w_gate, w_up, tm=tm, tf=tf)
    return ffn_down(act, *blocks(tm_down), w_down, x, tm=tm_down, tn=tn)


def _row_copy(src_hbm, dst_ref, sem, src_row, group, j):
    return pltpu.make_async_copy(src_hbm.at[pl.ds(src_row, 1), :], dst_ref.at[group, pl.ds(j, 1), :], sem)


def _gather_rows_kernel(na_ref, idx_ref, nxt_ref, h_hbm, o_ref, buf, sem):
    tm = buf.shape[1] * SUBLANES
    i = pl.program_id(0)
    slot = i % 2

    def rows(ids_ref, s, start):
        def body(g, c):
            for j in range(SUBLANES):
                cp = _row_copy(h_hbm, buf.at[s], sem.at[s], ids_ref[0, 0, g * SUBLANES + j], g, j)
                if start:
                    cp.start(priority=j % 2)
                else:
                    cp.wait()
            return c

        lax.fori_loop(0, tm // SUBLANES, body, 0)

    @pl.when(i == 0)
    def _():
        rows(idx_ref, 0, True)

    @pl.when(i + 1 < na_ref[0])
    def _():
        rows(nxt_ref, 1 - slot, True)

    @pl.when(i < na_ref[0])
    def _():
        rows(idx_ref, slot, False)
        o_ref[...] = buf[slot].reshape(o_ref.shape).astype(o_ref.dtype)

    @pl.when(i >= na_ref[0])
    def _():
        o_ref[...] = jnp.zeros_like(o_ref)


def gather_rows(h, row_token, n_active, *, tm):
    T, D = h.shape
    R = row_token.shape[0]
    ids = row_token.reshape(R // tm, 1, tm)
    return pl.pallas_call(
        _gather_rows_kernel,
        grid_spec=pltpu.PrefetchScalarGridSpec(
            num_scalar_prefetch=1,
            grid=(R // tm,),
            in_specs=[pl.BlockSpec((1, 1, tm), lambda i, na: (_block(i, na), 0, 0),
                                   memory_space=pltpu.SMEM),
                      pl.BlockSpec((1, 1, tm), lambda i, na: (_block(i + 1, na), 0, 0),
                                   memory_space=pltpu.SMEM),
                      pl.BlockSpec(memory_space=pl.ANY)],
            out_specs=pl.BlockSpec((tm, D), lambda i, na: (i, 0)),
            scratch_shapes=[pltpu.VMEM((2, tm // SUBLANES, SUBLANES, D), F32),
                            pltpu.SemaphoreType.DMA((2,))]),
        out_shape=jax.ShapeDtypeStruct((R, D), BF16),
        compiler_params=_params(("arbitrary",)),
        name="gather_rows",
    )(n_active, ids, ids, h)


def _combine_kernel(*refs, final):
    if final:
        dest_ref, x_ref, gate_ref, gain_ref, y_hbm, o_ref, buf0, buf1, sem = refs
    else:
        dest_ref, x_ref, gate_ref, y_hbm, o_ref, buf0, buf1, sem = refs
    tm = buf0.shape[0] * SUBLANES

    def copies(g, j):
        r = g * SUBLANES + j
        return (_row_copy(y_hbm, buf0, sem.at[0], dest_ref[0, 0, 2 * r], g, j),
                _row_copy(y_hbm, buf1, sem.at[1], dest_ref[0, 0, 2 * r + 1], g, j))

    def start(g, c):
        for j in range(SUBLANES):
            first, second = copies(g, j)
            first.start(priority=0)
            second.start(priority=1)
        return c

    def wait(g, c):
        for j in range(SUBLANES):
            for cp in copies(g, j):
                cp.wait()
        return c

    lax.fori_loop(0, tm // SUBLANES, start, 0)
    lax.fori_loop(0, tm // SUBLANES, wait, 0)
    gates = gate_ref[...]
    x = (x_ref[...] + gates[:, 0:1] * buf0[...].reshape(x_ref.shape)
         + gates[:, 1:2] * buf1[...].reshape(x_ref.shape))
    o_ref[...] = _rms(x, gain_ref[...]) if final else x


def combine_rows(x, y_rows, dest, gates, final_gain=None, *, tm=256):
    T, D = x.shape
    final = final_gain is not None
    row = pl.BlockSpec((tm, D), lambda i: (i, 0))
    in_specs = [pl.BlockSpec((1, 1, 2 * tm), lambda i: (i, 0, 0), memory_space=pltpu.SMEM),
                row, pl.BlockSpec((tm, gates.shape[1]), lambda i: (i, 0))]
    args = [dest.reshape(T // tm, 1, 2 * tm), x, gates]
    if final:
        in_specs.append(pl.BlockSpec((1, D), lambda i: (0, 0)))
        args.append(final_gain.reshape(1, D).astype(F32))
    in_specs.append(pl.BlockSpec(memory_space=pl.ANY))
    args.append(y_rows)
    return pl.pallas_call(
        functools.partial(_combine_kernel, final=final),
        grid=(T // tm,),
        in_specs=in_specs,
        out_specs=row,
        out_shape=jax.ShapeDtypeStruct((T, D), F32),
        scratch_shapes=[pltpu.VMEM((tm // SUBLANES, SUBLANES, D), F32),
                        pltpu.VMEM((tm // SUBLANES, SUBLANES, D), F32),
                        pltpu.SemaphoreType.DMA((2,))],
        compiler_params=_params(("arbitrary",)),
        name="combine_rows",
    )(*args)


def _rms_rows_kernel(x_ref, g_ref, o_ref):
    o_ref[...] = _rms(x_ref[...], g_ref[...])


def rms_rows(x, gain, *, tm=512):
    T, D = x.shape
    tm = min(tm, T)
    return pl.pallas_call(
        _rms_rows_kernel,
        grid=(T // tm,),
        in_specs=[pl.BlockSpec((tm, D), lambda i: (i, 0)), pl.BlockSpec((1, D), lambda i: (0, 0))],
        out_specs=pl.BlockSpec((tm, D), lambda i: (i, 0)),
        out_shape=jax.ShapeDtypeStruct((T, D), F32),
        compiler_params=_params(("parallel",)),
        name="final_norm",
    )(x, gain.reshape(1, D).astype(F32))


def moe_layer(h, x, gates, top_idx, w_gate, w_up, w_down, index, final_gain=None, *,
              tm=512, tm_down=512):
    T, D = x.shape
    E = w_gate.shape[1]
    w_gate, w_up, w_down = (w.reshape((-1,) + w.shape[2:]) for w in (w_gate, w_up, w_down))
    n_assign = 2 * T
    flat_expert = top_idx[:, :2].reshape(n_assign)
    onehot = (flat_expert[:, None] == jnp.arange(E, dtype=jnp.int32)[None, :]).astype(jnp.int32)
    csum = jnp.cumsum(onehot, axis=0)
    rank = jnp.take_along_axis(csum, flat_expert[:, None], axis=1)[:, 0] - 1
    counts = csum[-1]
    padded = (counts + tm - 1) // tm * tm
    padded_ends = jnp.cumsum(padded)
    dest = (padded_ends - padded)[flat_expert] + rank
    n_blocks = -(-n_assign // tm) + E
    row_token = jnp.zeros((n_blocks * tm,), jnp.int32).at[dest].set(
        jnp.arange(n_assign, dtype=jnp.int32) // 2)
    block_expert = jnp.minimum(
        jnp.searchsorted(padded_ends, jnp.arange(n_blocks, dtype=jnp.int32) * tm, side='right'),
        E - 1).astype(jnp.int32) + index * E
    n_active = (padded_ends[-1:] // tm).astype(jnp.int32)
    rows = gather_rows(h, row_token, n_active, tm=tm)
    act = ffn_up(rows, block_expert, n_active, w_gate, w_up, tm=tm, tf=1024)
    sub = tm // tm_down
    y_rows = ffn_down(act, jnp.repeat(block_expert, sub), n_active * sub, w_down, tm=tm_down, tn=512)
    return combine_rows(x, y_rows, dest.astype(jnp.int32), gates, final_gain)


def kernel(x, norm_mix, w_in, lb_logits, hgrn_norm, w_out, norm_ffn, dense_w_gate, dense_w_up,
           dense_w_down, moe_router, moe_w_gate, moe_w_up, moe_w_down, final_norm):
    B, S, D = x.shape
    depth = w_in.shape[0]
    lb_sm = jax.nn.softmax(lb_logits.astype(F32), axis=0)
    lower_bounds = jnp.cumsum(lb_sm, axis=0) - lb_sm[0:1]
    xt = x.reshape(B * S, D)
    for layer in range(depth):
        cfg = LAYER_TILING[layer % len(LAYER_TILING)]
        proj = norm_matmul(xt, norm_mix[layer], w_in, layer, **cfg["in_proj"])
        o_hgrn = hgrn2(proj, lower_bounds[layer], hgrn_norm[layer], batch=B, seq=S, **cfg["hgrn"])
        o_attn = dilated_attention(proj, batch=B, seq=S, col0=4 * HGRN_WIDTH, **cfg["attn"])
        i = layer // 2
        last = layer == depth - 1
        if layer % 2 == 0:
            xt, h = out_proj_norm(o_hgrn, o_attn, xt, w_out[layer].astype(BF16), norm_ffn[layer])
            xt = dense_ffn(h, xt, dense_w_gate, dense_w_up, dense_w_down, i, **cfg["ffn"])
            if last:
                xt = rms_rows(xt, final_norm)
        else:
            xt, h, gates, idx = out_proj_norm(o_hgrn, o_attn, xt, w_out[layer].astype(BF16),
                                              norm_ffn[layer], moe_router[i])
            xt = moe_layer(h, xt, gates, idx, moe_w_gate, moe_w_up, moe_w_down, i,
                           final_norm if last else None, **cfg["ffn"])
    return xt.reshape(B, S, D)
```

```python
import functools
import math

import jax
import jax.numpy as jnp
from jax import lax
from jax.experimental import pallas as pl
from jax.experimental.pallas import tpu as pltpu

F32 = jnp.float32
BF16 = jnp.bfloat16

HEAD_DIM = 128
N_HGRN_HEADS = 8
N_ATTN_HEADS = 8
HGRN_WIDTH = N_HGRN_HEADS * HEAD_DIM
ATTN_WIDTH = N_ATTN_HEADS * HEAD_DIM
HGRN_CHUNK = 64
HGRN_SUB = 8
HGRN_HEADS_PER_STEP = 8
DILATED_PATTERNS = ((128, 1), (512, 4), (2048, 16))
ATTN_CLASSES = 4
ATTN_PAD_N = 128
ATTN_PAD_M = 512
ATTN_UNITS = 32
SUBLANES = 8
OUT_PROJ_SUB_ROWS = 128
ROPE_THETA = 10000.0
N_EXPERTS = 8
NORM_EPS = 1e-6
VMEM_LIMIT = 60 * 1024 * 1024

LAYER_TILING = (
    dict(hgrn=dict(), out_proj=dict(), ffn=dict()),
    dict(hgrn=dict(chunks_per_iter=2), out_proj=dict(), ffn=dict()),
    dict(hgrn=dict(chunks_per_iter=2, tt=1024), out_proj=dict(sub_rows=256), ffn=dict()),
    dict(hgrn=dict(), out_proj=dict(sub_rows=64), ffn=dict(combine_tm=512)),
)

_NT = (((1,), (1,)), ((), ()))
_TN = (((0,), (0,)), ((), ()))


def _params(semantics):
    return pltpu.CompilerParams(dimension_semantics=semantics, vmem_limit_bytes=VMEM_LIMIT)


def _silu(z):
    return z * jax.nn.sigmoid(z)


def _rms(z, gain):
    ms = jnp.mean(z * z, axis=-1, keepdims=True)
    return z * lax.rsqrt(ms + NORM_EPS) * gain


def _in_proj_kernel(x_ref, g_ref, w_ref, o_ref, h_ref):
    @pl.when(pl.program_id(1) == 0)
    def _():
        h_ref[...] = _rms(x_ref[...], g_ref[...]).astype(BF16)

    o_ref[...] = jnp.dot(h_ref[...], w_ref[...].astype(BF16),
                         preferred_element_type=F32).astype(o_ref.dtype)


def norm_matmul(x, gain, w, index, *, tm=1024, tn=1024):
    T, D = x.shape
    N = w.shape[2]
    tm, tn = min(tm, T), min(tn, N)
    return pl.pallas_call(
        _in_proj_kernel,
        grid=(T // tm, N // tn),
        in_specs=[pl.BlockSpec((tm, D), lambda i, j: (i, 0)),
                  pl.BlockSpec((1, D), lambda i, j: (0, 0)),
                  pl.BlockSpec((None, D, tn), lambda i, j: (index, 0, j))],
        out_specs=pl.BlockSpec((tm, tn), lambda i, j: (i, j)),
        out_shape=jax.ShapeDtypeStruct((T, N), BF16),
        scratch_shapes=[pltpu.VMEM((tm, D), BF16)],
        compiler_params=_params(("parallel", "arbitrary")),
        name="in_proj",
    )(x, gain.reshape(1, D).astype(F32), w)


def _hgrn_kernel(q_ref, f_ref, i_ref, g_ref, lb_ref, gain_ref, o_ref,
                 st_ref, k_ref, tril_ref, select_ref, causal_ref, *, n_chunks, heads, chunks_per_iter):
    C, SUB = HGRN_CHUNK, HGRN_SUB

    @pl.when(pl.program_id(2) == 0)
    def _():
        st_ref[...] = jnp.zeros_like(st_ref)
        row = lax.broadcasted_iota(jnp.int32, (C, C), 0)
        col = lax.broadcasted_iota(jnp.int32, (C, C), 1)
        tril_ref[...] = (col <= row).astype(BF16)
        group = lax.shift_right_logical(
            lax.broadcasted_iota(jnp.int32, (SUB * HEAD_DIM, HEAD_DIM), 0), HEAD_DIM.bit_length() - 1)
        select_ref[...] = (group == lax.broadcasted_iota(jnp.int32, (SUB * HEAD_DIM, HEAD_DIM), 1)
                           ).astype(BF16)
        sub_row = lax.broadcasted_iota(jnp.int32, (SUB, SUB, HEAD_DIM), 1)
        key = lax.broadcasted_iota(jnp.int32, (SUB, SUB, HEAD_DIM), 0)
        causal_ref[...] = jnp.where(sub_row >= key, 0.0, -jnp.inf)

    pairs = []
    span = C
    while span > SUB:
        half = span // 2
        pairs += [(base + half, base + span, base, base + half) for base in range(0, C, span)]
        span = half

    def gates(hh, rows):
        lanes = slice(hh * HEAD_DIM, (hh + 1) * HEAD_DIM)
        lb = lb_ref[:, lanes]
        f = lb + (1.0 - lb) * jax.nn.sigmoid(f_ref[rows, lanes].astype(F32))
        g = jnp.log2(f)
        g_hi = g.astype(BF16)
        g_lo = (g - g_hi.astype(F32)).astype(BF16)
        tril = tril_ref[...]
        b = (jnp.dot(tril, g_hi, preferred_element_type=F32)
             + jnp.dot(tril, g_lo, preferred_element_type=F32))
        return dict(lanes=lanes, b=b, kk=1.0 - f, q=_silu(q_ref[rows, lanes].astype(F32)),
                    v=i_ref[rows, lanes].astype(F32))

    def scores(hh, slot, s):
        b, kk, q, v = s["b"], s["kk"], s["q"], s["v"]
        k_scr = k_ref.at[slot * heads + hh]
        k_scr[...] = jnp.log2(kk) - b
        s["v16"] = v.astype(BF16)
        b_end = b[C - 1:C, :]
        st = st_ref[hh]
        s["o"] = lax.dot_general((q * jnp.exp2(b)).astype(BF16), st.astype(BF16), _NT,
                                 preferred_element_type=F32)
        kd = (kk * jnp.exp2(b_end - b)).astype(BF16)
        st_ref[hh] = st * jnp.exp2(b_end) + lax.dot_general(s["v16"], kd, _TN,
                                                            preferred_element_type=F32)
        s["a_off"] = []
        for t0, t1, s0, s1 in pairs:
            b_mid = b[s1 - 1:s1, :]
            qa = (q[t0:t1] * jnp.exp2(b[t0:t1] - b_mid)).astype(BF16)
            ka = (kk[s0:s1] * jnp.exp2(b_mid - b[s0:s1])).astype(BF16)
            s["a_off"].append(lax.dot_general(qa, ka, _NT, preferred_element_type=F32))
        blocks = []
        for r in range(0, C, SUB):
            qs = q[r:r + SUB]
            bs = b[r:r + SUB]
            groups = []
            for j in range(SUB):
                d = bs + jnp.broadcast_to(k_scr[r + j:r + j + 1, :], (SUB, HEAD_DIM))
                if j:
                    d = d + causal_ref[j]
                groups.append(qs * jnp.exp2(d))
            blocks.append(jnp.concatenate(groups, axis=1))
        a = jnp.dot(jnp.concatenate(blocks, axis=0).astype(BF16), select_ref[...],
                    preferred_element_type=F32)
        s["a_diag"] = jnp.concatenate(
            [a[r:r + SUB] if r == 0 else pltpu.roll(a[r:r + SUB], r, 1) for r in range(0, C, SUB)], axis=0)

    def values(hh, s):
        parts = [jnp.zeros((SUB, HEAD_DIM), F32)] + [None] * (C // SUB - 1)

        def add(idx, piece):
            parts[idx] = piece if parts[idx] is None else parts[idx] + piece

        for (t0, t1, s0, s1), a in zip(pairs, s["a_off"]):
            blk = jnp.dot(a.astype(BF16), s["v16"][s0:s1], preferred_element_type=F32)
            for j in range((t1 - t0) // SUB):
                add(t0 // SUB + j, blk[j * SUB:(j + 1) * SUB])
        v_rows = jnp.concatenate([s["v16"], jnp.zeros((HEAD_DIM - C, HEAD_DIM), BF16)], axis=0)
        diag = jnp.dot(s["a_diag"].astype(BF16), v_rows, preferred_element_type=F32)
        s["o"] = s["o"] + diag + jnp.concatenate(parts, axis=0)

    def finish(s, rows):
        lanes = s["lanes"]
        o = _rms(s["o"], gain_ref[:, lanes]) * _silu(g_ref[rows, lanes].astype(F32))
        o_ref[rows, lanes] = o.astype(o_ref.dtype)

    def chunk(ci, carry):
        work = []
        for slot in range(chunks_per_iter):
            rows = pl.ds(pl.multiple_of((ci * chunks_per_iter + slot) * C, C), C)
            work += [(hh, slot, rows, gates(hh, rows)) for hh in range(heads)]
        for hh, slot, _, s in work:
            scores(hh, slot, s)
        for hh, _, _, s in work:
            values(hh, s)
        for _, _, rows, s in work:
            finish(s, rows)
        return carry

    lax.fori_loop(0, n_chunks // chunks_per_iter, chunk, 0)


def hgrn2(proj, lower_bound, out_gain, *, batch, seq, tt=512, heads=HGRN_HEADS_PER_STEP,
          chunks_per_iter=1):
    T = proj.shape[0]
    H = N_HGRN_HEADS // heads
    width = heads * HEAD_DIM
    tt = min(tt, seq)
    nt = seq // tt

    def sec(k):
        return pl.BlockSpec((tt, width), lambda b, h, c, k=k: (b * nt + c, k * H + h))

    vec = pl.BlockSpec((1, width), lambda b, h, c: (0, h))
    per_head = pltpu.VMEM((chunks_per_iter * heads, HGRN_CHUNK, HEAD_DIM), F32)
    return pl.pallas_call(
        functools.partial(_hgrn_kernel, n_chunks=tt // HGRN_CHUNK, heads=heads,
                          chunks_per_iter=chunks_per_iter),
        grid=(batch, H, nt),
        in_specs=[sec(0), sec(1), sec(2), sec(3), vec, vec],
        out_specs=pl.BlockSpec((tt, width), lambda b, h, c: (b * nt + c, h)),
        out_shape=jax.ShapeDtypeStruct((T, HGRN_WIDTH), BF16),
        scratch_shapes=[pltpu.VMEM((heads, HEAD_DIM, HEAD_DIM), F32), per_head,
                        pltpu.VMEM((HGRN_CHUNK, HGRN_CHUNK), BF16),
                        pltpu.VMEM((HGRN_SUB * HEAD_DIM, HEAD_DIM), BF16),
                        pltpu.VMEM((HGRN_SUB, HGRN_SUB, HEAD_DIM), F32)],
        compiler_params=_params(("parallel", "parallel", "arbitrary")),
        name="hgrn2",
    )(proj, proj, proj, proj,
      lower_bound.reshape(1, HGRN_WIDTH).astype(F32), out_gain.reshape(1, HGRN_WIDTH).astype(F32))


def _attn_kernel(q_ref, k_ref, v_ref, cos_ref, sin_ref, o_ref,
                 qn, kn, vn, qm, km, vm, o1, o2, o3, l1, l2, l3, bias_ref, *, seq, n_par):
    BLK = 128
    ROWS = 512
    NC = ATTN_CLASSES
    per_class = seq // NC
    scale = HEAD_DIM ** -0.5 * math.log2(math.e)

    def rope(i, carry):
        rows = pl.ds(pl.multiple_of(i * ROWS, ROWS), ROWS)
        c = cos_ref[rows, :]
        s = sin_ref[rows, :]
        q = q_ref[rows, :].astype(F32)
        k = k_ref[rows, :].astype(F32)
        qn[rows, :] = (q * c + pltpu.roll(q, HEAD_DIM // 2, 1) * s) * scale
        padded = pl.ds(pl.multiple_of(i * ROWS, ROWS) + ATTN_PAD_N, ROWS)
        kn[padded, :] = k * c + pltpu.roll(k, HEAD_DIM // 2, 1) * s
        vn[padded, :] = v_ref[rows, :].astype(F32)
        return carry

    lax.fori_loop(0, seq // ROWS, rope, 0)

    kn[0:ATTN_PAD_N, :] = jnp.zeros((ATTN_PAD_N, HEAD_DIM), F32)
    vn[0:ATTN_PAD_N, :] = jnp.zeros((ATTN_PAD_N, HEAD_DIM), F32)
    for c in range(NC):
        base = c * (ATTN_PAD_M + per_class)
        km[base:base + ATTN_PAD_M, :] = jnp.zeros((ATTN_PAD_M, HEAD_DIM), F32)
        vm[base:base + ATTN_PAD_M, :] = jnp.zeros((ATTN_PAD_M, HEAD_DIM), F32)

    def regroup(i, carry):
        j0 = pl.multiple_of(i * ROWS, ROWS)
        for c in range(NC):
            src = pl.ds(NC * j0 + c, ROWS, stride=NC)
            qm[pl.ds(c * per_class + j0, ROWS), :] = qn[src, :]
            dst = pl.ds(c * (ATTN_PAD_M + per_class) + ATTN_PAD_M + j0, ROWS)
            km[dst, :] = kn[pl.ds(NC * j0 + c + ATTN_PAD_N, ROWS, stride=NC), :]
            vm[dst, :] = vn[pl.ds(NC * j0 + c + ATTN_PAD_N, ROWS, stride=NC), :]
        return carry

    lax.fori_loop(0, per_class // ROWS, regroup, 0)

    qi = lax.broadcasted_iota(jnp.int32, (BLK, 2 * BLK), 0)
    kj = lax.broadcasted_iota(jnp.int32, (BLK, 2 * BLK), 1)
    band = (kj >= qi) & (kj <= qi + BLK)
    bias_ref[0] = jnp.where(band & (kj >= BLK), 0.0, -jnp.inf)
    bias_ref[1] = jnp.where(band, 0.0, -jnp.inf)
    ones = jnp.ones((2 * BLK, HEAD_DIM), BF16)

    def ds(start, size, stride):
        return pl.ds(start, size) if stride == 1 else pl.ds(start, size, stride=stride)

    def unit_rows(d, u):
        if d == 1:
            return u, pl.ds(u * BLK, BLK), pl.ds(u * BLK + (ATTN_PAD_N - BLK), 2 * BLK), pl.ds(u * BLK, BLK)
        step = d // NC
        nb = per_class // (BLK * step)
        n = u & (nb - 1)
        r = lax.shift_right_logical(u, nb.bit_length() - 1) & (step - 1)
        c = lax.shift_right_logical(u, (nb * step).bit_length() - 1)
        j0 = n * (BLK * step) + r
        return (n, ds(c * per_class + j0, BLK, step),
                ds(c * (ATTN_PAD_M + per_class) + (ATTN_PAD_M - BLK * step) + j0, 2 * BLK, step),
                pl.ds(NC * j0 + c, BLK, stride=d))

    def qk(d, u):
        q_src, k_src, v_src = (qn, kn, vn) if d == 1 else (qm, km, vm)
        n, q_rows, k_rows, out_rows = unit_rows(d, u)
        q = q_src[q_rows, :].astype(BF16)
        k = k_src[k_rows, :].astype(BF16)
        v = v_src[k_rows, :].astype(BF16)
        s = lax.dot_general(q, k, _NT, preferred_element_type=F32)
        return dict(n=n, out_rows=out_rows, s=s, v=v)

    def pv(t):
        s = t["s"] + bias_ref[jnp.minimum(t["n"], 1)]
        t["m"] = jnp.max(s, axis=-1, keepdims=True)
        p = jnp.exp2(s - t["m"]).astype(BF16)
        t["acc"] = jnp.dot(p, jnp.concatenate([t["v"], ones], axis=1), preferred_element_type=F32)

    def store(t, o_out, l_out):
        den = t["acc"][:, HEAD_DIM:]
        o_out[t["out_rows"], :] = t["acc"][:, :HEAD_DIM] / den
        l_out[t["out_rows"], :] = t["m"] + jnp.log2(den)

    for (window, d), o_out, l_out in zip(DILATED_PATTERNS, (o1, o2, o3), (l1, l2, l3)):
        assert window == BLK * d and (d == 1 or d % NC == 0)
        assert BLK * d <= (ATTN_PAD_N if d == 1 else ATTN_PAD_M * NC) and per_class % (BLK * max(d // NC, 1)) == 0
        n_units = seq // BLK
        assert n_units % n_par == 0

        def units(it, carry, d=d, o_out=o_out, l_out=l_out):
            ts = [qk(d, it * n_par + j) for j in range(n_par)]
            for t in ts:
                pv(t)
            for t in ts:
                store(t, o_out, l_out)
            return carry

        lax.fori_loop(0, n_units // n_par, units, 0)

    def combine(i, carry):
        rows = pl.ds(pl.multiple_of(i * ROWS, ROWS), ROWS)
        a1, a2, a3 = l1[rows, :], l2[rows, :], l3[rows, :]
        m = jnp.maximum(jnp.maximum(a1, a2), a3)
        e1, e2, e3 = jnp.exp2(a1 - m), jnp.exp2(a2 - m), jnp.exp2(a3 - m)
        out = (e1 * o1[rows, :] + e2 * o2[rows, :] + e3 * o3[rows, :]) / (e1 + e2 + e3)
        o_ref[rows, :] = out.astype(o_ref.dtype)
        return carry

    lax.fori_loop(0, seq // ROWS, combine, 0)


def _rope_tables(seq):
    half = HEAD_DIM // 2
    inv_freq = ROPE_THETA ** (-jnp.arange(half, dtype=F32) / half)
    ang = jnp.arange(seq, dtype=F32)[:, None] * inv_freq[None, :]
    cos, sin = jnp.cos(ang), jnp.sin(ang)
    return jnp.concatenate([cos, cos], axis=-1), jnp.concatenate([-sin, sin], axis=-1)


def dilated_attention(proj, *, batch, seq, col0, n_par=ATTN_UNITS):
    T = proj.shape[0]
    H = N_ATTN_HEADS
    c0 = col0 // HEAD_DIM
    cos, sin = _rope_tables(seq)

    def sec(k):
        return pl.BlockSpec((seq, HEAD_DIM), lambda b, h, k=k: (b, c0 + k * H + h))

    table = pl.BlockSpec((seq, HEAD_DIM), lambda b, h: (0, 0), pipeline_mode=pl.Buffered(1))
    slab = pltpu.VMEM((seq, HEAD_DIM), F32)
    padded_n = pltpu.VMEM((ATTN_PAD_N + seq, HEAD_DIM), F32)
    padded_m = pltpu.VMEM((ATTN_CLASSES * ATTN_PAD_M + seq, HEAD_DIM), F32)
    return pl.pallas_call(
        functools.partial(_attn_kernel, seq=seq, n_par=n_par),
        grid=(batch, H),
        in_specs=[sec(0), sec(1), sec(2), table, table],
        out_specs=pl.BlockSpec((seq, HEAD_DIM), lambda b, h: (b, h)),
        out_shape=jax.ShapeDtypeStruct((T, ATTN_WIDTH), BF16),
        scratch_shapes=[slab, padded_n, padded_n, slab, padded_m, padded_m] + [slab] * 6
                       + [pltpu.VMEM((2, HEAD_DIM, 2 * HEAD_DIM), F32)],
        compiler_params=_params(("parallel", "parallel")),
        name="dilated_attn",
    )(proj, proj, proj, cos, sin)


def _out_proj_kernel(*refs, route, sub_rows):
    if route:
        oh_ref, oa_ref, x_ref, w_ref, g_ref, wr_ref, xo_ref, h_ref, gate_ref, idx_ref = refs
    else:
        oh_ref, oa_ref, x_ref, w_ref, g_ref, xo_ref, h_ref = refs
    hw = oh_ref.shape[1]
    tm = x_ref.shape[0]
    sub = min(tm, sub_rows)
    for r0 in range(0, tm, sub):
        rows = slice(r0, r0 + sub)
        y = (jnp.dot(oh_ref[rows, :], w_ref[:hw, :], preferred_element_type=F32)
             + jnp.dot(oa_ref[rows, :], w_ref[hw:, :], preferred_element_type=F32))
        x = x_ref[rows, :] + y
        xo_ref[rows, :] = x
        h = _rms(x, g_ref[...])
        h_ref[rows, :] = h.astype(h_ref.dtype)
        if route:
            h_hi = h.astype(BF16)
            h_lo = (h - h_hi.astype(F32)).astype(BF16)
            logits = (jnp.dot(h_hi, wr_ref[0], preferred_element_type=F32)
                      + jnp.dot(h_lo, wr_ref[0], preferred_element_type=F32)
                      + jnp.dot(h_hi, wr_ref[1], preferred_element_type=F32))
            lane = lax.broadcasted_iota(jnp.int32, logits.shape, 1)
            lg = jnp.where(lane < N_EXPERTS, logits, -jnp.inf)
            m1 = jnp.max(lg, axis=-1, keepdims=True)
            i1 = jnp.min(jnp.where(lg == m1, lane, HEAD_DIM), axis=-1, keepdims=True)
            lg = jnp.where(lane == i1, -jnp.inf, lg)
            m2 = jnp.max(lg, axis=-1, keepdims=True)
            i2 = jnp.min(jnp.where(lg == m2, lane, HEAD_DIM), axis=-1, keepdims=True)
            e = jnp.exp(m2 - m1)
            gate_ref[rows, :] = jnp.where(lane == 0, 1.0 / (1.0 + e),
                                          jnp.where(lane == 1, e / (1.0 + e), 0.0))
            idx_ref[rows, :] = jnp.where(lane == 0, i1, jnp.where(lane == 1, i2, 0))


def out_proj_norm(o_hgrn, o_attn, x, w, gain, w_router=None, *, tm=512, sub_rows=OUT_PROJ_SUB_ROWS):
    T, D = x.shape
    tm = min(tm, T)
    route = w_router is not None
    row = lambda width: pl.BlockSpec((tm, width), lambda i: (i, 0))
    whole = lambda a: pl.BlockSpec(a.shape, lambda i: (0, 0), pipeline_mode=pl.Buffered(1))
    gain2 = gain.reshape(1, D).astype(F32)
    args = [o_hgrn, o_attn, x, w, gain2]
    in_specs = [row(o_hgrn.shape[1]), row(o_attn.shape[1]), row(D), whole(w), whole(gain2)]
    out_specs = [row(D), row(D)]
    out_shape = [jax.ShapeDtypeStruct((T, D), F32), jax.ShapeDtypeStruct((T, D), F32 if route else BF16)]
    if route:
        wr = jnp.zeros((D, HEAD_DIM), F32).at[:, :N_EXPERTS].set(w_router.astype(F32))
        wr_hi = wr.astype(BF16)
        wr = jnp.stack([wr_hi, (wr - wr_hi.astype(F32)).astype(BF16)])
        args.append(wr)
        in_specs.append(pl.BlockSpec(wr.shape, lambda i: (0, 0, 0), pipeline_mode=pl.Buffered(1)))
        out_specs += [row(HEAD_DIM), row(HEAD_DIM)]
        out_shape += [jax.ShapeDtypeStruct((T, HEAD_DIM), F32),
                      jax.ShapeDtypeStruct((T, HEAD_DIM), jnp.int32)]
    return pl.pallas_call(
        functools.partial(_out_proj_kernel, route=route, sub_rows=sub_rows),
        grid=(T // tm,),
        in_specs=in_specs,
        out_specs=out_specs,
        out_shape=out_shape,
        compiler_params=_params(("parallel",)),
        name="out_proj_route" if route else "out_proj",
    )(*args)


def _fresh_weights(be_ref, i):
    return (i == 0) | (be_ref[i] != be_ref[jnp.maximum(i - 1, 0)])


def _ffn_up_kernel(be_ref, na_ref, x_ref, wg_ref, wu_ref, a_ref, wg16, wu16):
    i = pl.program_id(1)
    active = i < na_ref[0]

    @pl.when(active & _fresh_weights(be_ref, i))
    def _():
        wg16[...] = wg_ref[...].astype(BF16)
        wu16[...] = wu_ref[...].astype(BF16)

    @pl.when(active)
    def _():
        x = x_ref[...]
        g = jnp.dot(x, wg16[...], preferred_element_type=F32)
        u = jnp.dot(x, wu16[...], preferred_element_type=F32)
        a_ref[...] = (_silu(g) * u).astype(a_ref.dtype)

    @pl.when(jnp.logical_not(active))
    def _():
        a_ref[...] = jnp.zeros_like(a_ref)


def _ffn_down_kernel(*refs, residual):
    if residual:
        be_ref, na_ref, a_ref, wd_ref, x_ref, o_ref, wd16 = refs
    else:
        be_ref, na_ref, a_ref, wd_ref, o_ref, wd16 = refs
    i = pl.program_id(1)
    active = i < na_ref[0]

    @pl.when(active & _fresh_weights(be_ref, i))
    def _():
        wd16[...] = wd_ref[...].astype(BF16)

    @pl.when(active)
    def _():
        y = jnp.dot(a_ref[...], wd16[...], preferred_element_type=F32)
        o_ref[...] = x_ref[...] + y if residual else y

    @pl.when(jnp.logical_not(active))
    def _():
        o_ref[...] = jnp.zeros_like(o_ref)


def _block(i, na):
    return jnp.minimum(i, na[0] - 1)


def ffn_up(rows, block_expert, n_active, w_gate, w_up, *, tm, tf):
    R, D = rows.shape
    F = w_gate.shape[2]
    tf = min(tf, F)
    wspec =pl.BlockSpec((None, D, tf), lambda f, i, be, na: (be[_block(i, na)], 0, f))
    return pl.pallas_call(
        _ffn_up_kernel,
        grid_spec=pltpu.PrefetchScalarGridSpec(
            num_scalar_prefetch=2,
            grid=(F // tf, R // tm),
            in_specs=[pl.BlockSpec((tm, D), lambda f, i, be, na: (_block(i, na), 0)), wspec, wspec],
            out_specs=pl.BlockSpec((tm, tf), lambda f, i, be, na: (i, f)),
            scratch_shapes=[pltpu.VMEM((D, tf), BF16), pltpu.VMEM((D, tf), BF16)]),
        out_shape=jax.ShapeDtypeStruct((R, F), BF16),
        compiler_params=_params(("arbitrary", "arbitrary")),
        name="ffn_up",
    )(block_expert, n_active, rows, w_gate, w_up)


def ffn_down(act, block_expert, n_active, w_down, x=None, *, tm, tn):
    R, F = act.shape
    D = w_down.shape[2]
    tn = min(tn, D)
    residual = x is not None
    tile = pl.BlockSpec((tm, tn), lambda n, i, be, na: (i, n))
    in_specs = [pl.BlockSpec((tm, F), lambda n, i, be, na: (_block(i, na), 0)),
                pl.BlockSpec((None, F, tn), lambda n, i, be, na: (be[_block(i, na)], 0, n))]
    args = [act, w_down]
    if residual:
        in_specs.append(tile)
        args.append(x)
    return pl.pallas_call(
        functools.partial(_ffn_down_kernel, residual=residual),
        grid_spec=pltpu.PrefetchScalarGridSpec(
            num_scalar_prefetch=2,
            grid=(D // tn, R // tm),
            in_specs=in_specs,
            out_specs=tile,
            scratch_shapes=[pltpu.VMEM((F, tn), BF16)]),
        out_shape=jax.ShapeDtypeStruct((R, D), F32),
        compiler_params=_params(("arbitrary", "arbitrary")),
        name="ffn_down",
    )(block_expert, n_active, *args)


def dense_ffn(h, x, w_gate, w_up, w_down, index, *, tm=1024, tf=512, tm_down=512, tn=512):
    T = x.shape[0]

    def blocks(t):
        return jnp.full((T // t,), index, jnp.int32), jnp.full((1,), T // t, jnp.int32)

    act = ffn_up(h, *blocks(tm), w_gate, w_up, tm=tm, tf=tf)
    return ffn_down(act, *blocks(tm_down), w_down, x, tm=tm_down, tn=tn)


def _row_copy(src_hbm, dst_ref, sem, src_row, group, j):
    return pltpu.make_async_copy(src_hbm.at[pl.ds(src_row, 1), :], dst_ref.at[group, pl.ds(j, 1), :], sem)


def _gather_rows_kernel(na_ref, idx_ref, nxt_ref, h_hbm, o_ref, buf, sem):
    tm = buf.shape[1] * SUBLANES
    i = pl.program_id(0)
    slot = i % 2

    def rows(ids_ref, s, start):
        def body(g, c):
            for j in range(SUBLANES):
                cp = _row_copy(h_hbm, buf.at[s], sem.at[s], ids_ref[0, 0, g * SUBLANES + j], g, j)
                if start:
                    cp.start(priority=j % 2)
                else:
                    cp.wait()
            return c

        lax.fori_loop(0, tm // SUBLANES, body, 0)

    @pl.when(i == 0)
    def _():
        rows(idx_ref, 0, True)

    @pl.when(i + 1 < na_ref[0])
    def _():
        rows(nxt_ref, 1 - slot, True)

    @pl.when(i < na_ref[0])
    def _():
        rows(idx_ref, slot, False)
        o_ref[...] = buf[slot].reshape(o_ref.shape).astype(o_ref.dtype)

    @pl.when(i >= na_ref[0])
    def _():
        o_ref[...] = jnp.zeros_like(o_ref)


def gather_rows(h, row_token, n_active, *, tm):
    T, D = h.shape
    R = row_token.shape[0]
    ids = row_token.reshape(R // tm, 1, tm)
    return pl.pallas_call(
        _gather_rows_kernel,
        grid_spec=pltpu.PrefetchScalarGridSpec(
            num_scalar_prefetch=1,
            grid=(R // tm,),
            in_specs=[pl.BlockSpec((1, 1, tm), lambda i, na: (_block(i, na), 0, 0),
                                   memory_space=pltpu.SMEM),
                      pl.BlockSpec((1, 1, tm), lambda i, na: (_block(i + 1, na), 0, 0),
                                   memory_space=pltpu.SMEM),
                      pl.BlockSpec(memory_space=pl.ANY)],
            out_specs=pl.BlockSpec((tm, D), lambda i, na: (i, 0)),
            scratch_shapes=[pltpu.VMEM((2, tm // SUBLANES, SUBLANES, D), F32),
                            pltpu.SemaphoreType.DMA((2,))]),
        out_shape=jax.ShapeDtypeStruct((R, D), BF16),
        compiler_params=_params(("arbitrary",)),
        name="gather_rows",
    )(n_active, ids, ids, h)


def _combine_kernel(*refs, final):
    if final:
        dest_ref, x_ref, gate_ref, gain_ref, y_hbm, o_ref, buf0, buf1, sem = refs
    else:
        dest_ref, x_ref, gate_ref, y_hbm, o_ref, buf0, buf1, sem = refs
    tm = buf0.shape[0] * SUBLANES

    def copies(g, j):
        r = g * SUBLANES + j
        return (_row_copy(y_hbm, buf0, sem.at[0], dest_ref[0, 0, 2 * r], g, j),
                _row_copy(y_hbm, buf1, sem.at[1], dest_ref[0, 0, 2 * r + 1], g, j))

    def start(g, c):
        for j in range(SUBLANES):
            first, second = copies(g, j)
            first.start(priority=0)
            second.start(priority=1)
        return c

    def wait(g, c):
        for j in range(SUBLANES):
            for cp in copies(g, j):
                cp.wait()
        return c

    lax.fori_loop(0, tm // SUBLANES, start, 0)
    lax.fori_loop(0, tm // SUBLANES, wait, 0)
    gates = gate_ref[...]
    x = (x_ref[...] + gates[:, 0:1] * buf0[...].reshape(x_ref.shape)
         + gates[:, 1:2] * buf1[...].reshape(x_ref.shape))
    o_ref[...] = _rms(x, gain_ref[...]) if final else x


def combine_rows(x, y_rows, dest, gates, final_gain=None, *, tm=256):
    T, D = x.shape
    final = final_gain is not None
    row = pl.BlockSpec((tm, D), lambda i: (i, 0))
    in_specs = [pl.BlockSpec((1, 1, 2 * tm), lambda i: (i, 0, 0), memory_space=pltpu.SMEM),
                row, pl.BlockSpec((tm, gates.shape[1]), lambda i: (i, 0))]
    args = [dest.reshape(T // tm, 1, 2 * tm), x, gates]
    if final:
        in_specs.append(pl.BlockSpec((1, D), lambda i: (0, 0)))
        args.append(final_gain.reshape(1, D).astype(F32))
    in_specs.append(pl.BlockSpec(memory_space=pl.ANY))
    args.append(y_rows)
    return pl.pallas_call(
        functools.partial(_combine_kernel, final=final),
        grid=(T // tm,),
        in_specs=in_specs,
        out_specs=row,
        out_shape=jax.ShapeDtypeStruct((T, D), F32),
        scratch_shapes=[pltpu.VMEM((tm // SUBLANES, SUBLANES, D), F32),
                        pltpu.VMEM((tm // SUBLANES, SUBLANES, D), F32),
                        pltpu.SemaphoreType.DMA((2,))],
        compiler_params=_params(("arbitrary",)),
        name="combine_rows",
    )(*args)


def _rms_rows_kernel(x_ref, g_ref, o_ref):
    o_ref[...] = _rms(x_ref[...], g_ref[...])


def rms_rows(x, gain, *, tm=512):
    T, D = x.shape
    tm = min(tm, T)
    return pl.pallas_call(
        _rms_rows_kernel,
        grid=(T // tm,),
        in_specs=[pl.BlockSpec((tm, D), lambda i: (i, 0)), pl.BlockSpec((1, D), lambda i: (0, 0))],
        out_specs=pl.BlockSpec((tm, D), lambda i: (i, 0)),
        out_shape=jax.ShapeDtypeStruct((T, D), F32),
        compiler_params=_params(("parallel",)),
        name="final_norm",
    )(x, gain.reshape(1, D).astype(F32))


def moe_layer(h, x, gates, top_idx, w_gate, w_up, w_down, index, final_gain=None, *,
              tm=512, tm_down=512, combine_tm=256):
    T, D = x.shape
    E = w_gate.shape[1]
    w_gate, w_up, w_down = (w.reshape((-1,) + w.shape[2:]) for w in (w_gate, w_up, w_down))
    n_assign = 2 * T
    flat_expert = top_idx[:, :2].reshape(n_assign)
    onehot = (flat_expert[:, None] == jnp.arange(E, dtype=jnp.int32)[None, :]).astype(jnp.int32)
    csum = jnp.cumsum(onehot, axis=0)
    rank = jnp.take_along_axis(csum, flat_expert[:, None], axis=1)[:, 0] - 1
    counts = csum[-1]
    padded = (counts + tm - 1) // tm * tm
    padded_ends = jnp.cumsum(padded)
    dest = (padded_ends - padded)[flat_expert] + rank
    n_blocks = -(-n_assign // tm) + E
    row_token = jnp.zeros((n_blocks * tm,), jnp.int32).at[dest].set(
        jnp.arange(n_assign, dtype=jnp.int32) // 2)
    block_expert = jnp.minimum(
        jnp.searchsorted(padded_ends, jnp.arange(n_blocks, dtype=jnp.int32) * tm, side='right'),
        E - 1).astype(jnp.int32) + index * E
    n_active = (padded_ends[-1:] // tm).astype(jnp.int32)
    rows = gather_rows(h, row_token, n_active, tm=tm)
    act = ffn_up(rows, block_expert, n_active, w_gate, w_up, tm=tm, tf=1024)
    sub = tm // tm_down
    y_rows = ffn_down(act, jnp.repeat(block_expert, sub), n_active * sub, w_down, tm=tm_down, tn=512)
    return combine_rows(x, y_rows, dest.astype(jnp.int32), gates, final_gain, tm=combine_tm)


def kernel(x, norm_mix, w_in, lb_logits, hgrn_norm, w_out, norm_ffn, dense_w_gate, dense_w_up,
           dense_w_down, moe_router, moe_w_gate, moe_w_up, moe_w_down, final_norm):
    B, S, D = x.shape
    depth = w_in.shape[0]
    lb_sm = jax.nn.softmax(lb_logits.astype(F32), axis=0)
    lower_bounds = jnp.cumsum(lb_sm, axis=0) - lb_sm[0:1]
    xt = x.reshape(B * S, D)
    for layer in range(depth):
        cfg = LAYER_TILING[layer % len(LAYER_TILING)]
        proj = norm_matmul(xt, norm_mix[layer], w_in, layer)
        o_hgrn = hgrn2(proj, lower_bounds[layer], hgrn_norm[layer], batch=B, seq=S, **cfg["hgrn"])
        o_attn = dilated_attention(proj, batch=B, seq=S, col0=4 * HGRN_WIDTH)
        i = layer // 2
        last = layer == depth - 1
        if layer % 2 == 0:
            xt, h = out_proj_norm(o_hgrn, o_attn, xt, w_out[layer].astype(BF16), norm_ffn[layer],
                                  **cfg["out_proj"])
            xt = dense_ffn(h, xt, dense_w_gate, dense_w_up, dense_w_down, i, **cfg["ffn"])
            if last:
                xt = rms_rows(xt, final_norm)
        else:
            xt, h, gates, idx = out_proj_norm(o_hgrn, o_attn, xt, w_out[layer].astype(BF16),
                                              norm_ffn[layer], moe_router[i], **cfg["out_proj"])
            xt = moe_layer(h, xt, gates, idx, moe_w_gate, moe_w_up, moe_w_down, i,
                           final_norm if last else None, **cfg["ffn"])
    return xt.reshape(B, S, D)
```

```python
import functools
import math

import jax
import jax.numpy as jnp
from jax import lax
from jax.experimental import pallas as pl
from jax.experimental.pallas import tpu as pltpu

F32 = jnp.float32
BF16 = jnp.bfloat16

HEAD_DIM = 128
N_HGRN_HEADS = 8
N_ATTN_HEADS = 8
HGRN_WIDTH = N_HGRN_HEADS * HEAD_DIM
ATTN_WIDTH = N_ATTN_HEADS * HEAD_DIM
HGRN_CHUNK = 64
HGRN_SUB = 8
HGRN_HEADS_PER_STEP = 8
DILATED_PATTERNS = ((128, 1), (512, 4), (2048, 16))
ATTN_CLASSES = 4
ATTN_PAD_N = 128
ATTN_PAD_M = 512
ATTN_UNITS = 32
SUBLANES = 8
OUT_PROJ_SUB_ROWS = 128
ROPE_THETA = 10000.0
N_EXPERTS = 8
NORM_EPS = 1e-6
VMEM_LIMIT = 60 * 1024 * 1024

LAYER_TILING = (
    dict(hgrn=dict(chunks_per_iter=2), out_proj=dict(sub_rows=256), ffn=dict()),
    dict(hgrn=dict(chunks_per_iter=2), out_proj=dict(sub_rows=128), ffn=dict(combine_tm=512)),
    dict(hgrn=dict(chunks_per_iter=4), out_proj=dict(sub_rows=512), ffn=dict()),
    dict(hgrn=dict(chunks_per_iter=2), out_proj=dict(sub_rows=256), ffn=dict(combine_tm=1024)),
)

_NT = (((1,), (1,)), ((), ()))
_TN = (((0,), (0,)), ((), ()))


def _params(semantics):
    return pltpu.CompilerParams(dimension_semantics=semantics, vmem_limit_bytes=VMEM_LIMIT)


def _silu(z):
    return z * jax.nn.sigmoid(z)


def _rms(z, gain):
    ms = jnp.mean(z * z, axis=-1, keepdims=True)
    return z * lax.rsqrt(ms + NORM_EPS) * gain


def _in_proj_kernel(x_ref, g_ref, w_ref, o_ref, h_ref):
    @pl.when(pl.program_id(1) == 0)
    def _():
        h_ref[...] = _rms(x_ref[...], g_ref[...]).astype(BF16)

    o_ref[...] = jnp.dot(h_ref[...], w_ref[...].astype(BF16),
                         preferred_element_type=F32).astype(o_ref.dtype)


def norm_matmul(x, gain, w, index, *, tm=1024, tn=1024):
    T, D = x.shape
    N = w.shape[2]
    tm, tn = min(tm, T), min(tn, N)
    return pl.pallas_call(
        _in_proj_kernel,
        grid=(T // tm, N // tn),
        in_specs=[pl.BlockSpec((tm, D), lambda i, j: (i, 0)),
                  pl.BlockSpec((1, D), lambda i, j: (0, 0)),
                  pl.BlockSpec((None, D, tn), lambda i, j: (index, 0, j))],
        out_specs=pl.BlockSpec((tm, tn), lambda i, j: (i, j)),
        out_shape=jax.ShapeDtypeStruct((T, N), BF16),
        scratch_shapes=[pltpu.VMEM((tm, D), BF16)],
        compiler_params=_params(("parallel", "arbitrary")),
        name="in_proj",
    )(x, gain.reshape(1, D).astype(F32), w)


def _hgrn_kernel(q_ref, f_ref, i_ref, g_ref, lb_ref, gain_ref, o_ref,
                 st_ref, k_ref, tril_ref, select_ref, causal_ref, *, n_chunks, heads, chunks_per_iter):
    C, SUB = HGRN_CHUNK, HGRN_SUB

    @pl.when(pl.program_id(2) == 0)
    def _():
        st_ref[...] = jnp.zeros_like(st_ref)
        row = lax.broadcasted_iota(jnp.int32, (C, C), 0)
        col = lax.broadcasted_iota(jnp.int32, (C, C), 1)
        tril_ref[...] = (col <= row).astype(BF16)
        group = lax.shift_right_logical(
            lax.broadcasted_iota(jnp.int32, (SUB * HEAD_DIM, HEAD_DIM), 0), HEAD_DIM.bit_length() - 1)
        select_ref[...] = (group == lax.broadcasted_iota(jnp.int32, (SUB * HEAD_DIM, HEAD_DIM), 1)
                           ).astype(BF16)
        sub_row = lax.broadcasted_iota(jnp.int32, (SUB, SUB, HEAD_DIM), 1)
        key = lax.broadcasted_iota(jnp.int32, (SUB, SUB, HEAD_DIM), 0)
        causal_ref[...] = jnp.where(sub_row >= key, 0.0, -jnp.inf)

    pairs = []
    span = C
    while span > SUB:
        half = span // 2
        pairs += [(base + half, base + span, base, base + half) for base in range(0, C, span)]
        span = half

    def gates(hh, rows):
        lanes = slice(hh * HEAD_DIM, (hh + 1) * HEAD_DIM)
        lb = lb_ref[:, lanes]
        f = lb + (1.0 - lb) * jax.nn.sigmoid(f_ref[rows, lanes].astype(F32))
        g = jnp.log2(f)
        g_hi = g.astype(BF16)
        g_lo = (g - g_hi.astype(F32)).astype(BF16)
        tril = tril_ref[...]
        b = (jnp.dot(tril, g_hi, preferred_element_type=F32)
             + jnp.dot(tril, g_lo, preferred_element_type=F32))
        return dict(lanes=lanes, b=b, kk=1.0 - f, q=_silu(q_ref[rows, lanes].astype(F32)),
                    v=i_ref[rows, lanes].astype(F32))

    def scores(hh, slot, s):
        b, kk, q, v = s["b"], s["kk"], s["q"], s["v"]
        k_scr = k_ref.at[slot * heads + hh]
        k_scr[...] = jnp.log2(kk) - b
        s["v16"] = v.astype(BF16)
        b_end = b[C - 1:C, :]
        st = st_ref[hh]
        s["o"] = lax.dot_general((q * jnp.exp2(b)).astype(BF16), st.astype(BF16), _NT,
                                 preferred_element_type=F32)
        kd = (kk * jnp.exp2(b_end - b)).astype(BF16)
        st_ref[hh] = st * jnp.exp2(b_end) + lax.dot_general(s["v16"], kd, _TN,
                                                            preferred_element_type=F32)
        s["a_off"] = []
        for t0, t1, s0, s1 in pairs:
            b_mid = b[s1 - 1:s1, :]
            qa = (q[t0:t1] * jnp.exp2(b[t0:t1] - b_mid)).astype(BF16)
            ka = (kk[s0:s1] * jnp.exp2(b_mid - b[s0:s1])).astype(BF16)
            s["a_off"].append(lax.dot_general(qa, ka, _NT, preferred_element_type=F32))
        blocks = []
        for r in range(0, C, SUB):
            qs = q[r:r + SUB]
            bs = b[r:r + SUB]
            groups = []
            for j in range(SUB):
                d = bs + jnp.broadcast_to(k_scr[r + j:r + j + 1, :], (SUB, HEAD_DIM))
                if j:
                    d = d + causal_ref[j]
                groups.append(qs * jnp.exp2(d))
            blocks.append(jnp.concatenate(groups, axis=1))
        a = jnp.dot(jnp.concatenate(blocks, axis=0).astype(BF16), select_ref[...],
                    preferred_element_type=F32)
        s["a_diag"] = jnp.concatenate(
            [a[r:r + SUB] if r == 0 else pltpu.roll(a[r:r + SUB], r, 1) for r in range(0, C, SUB)], axis=0)

    def values(hh, s):
        parts = [jnp.zeros((SUB, HEAD_DIM), F32)] + [None] * (C // SUB - 1)

        def add(idx, piece):
            parts[idx] = piece if parts[idx] is None else parts[idx] + piece

        for (t0, t1, s0, s1), a in zip(pairs, s["a_off"]):
            blk = jnp.dot(a.astype(BF16), s["v16"][s0:s1], preferred_element_type=F32)
            for j in range((t1 - t0) // SUB):
                add(t0 // SUB + j, blk[j * SUB:(j + 1) * SUB])
        v_rows = jnp.concatenate([s["v16"], jnp.zeros((HEAD_DIM - C, HEAD_DIM), BF16)], axis=0)
        diag = jnp.dot(s["a_diag"].astype(BF16), v_rows, preferred_element_type=F32)
        s["o"] = s["o"] + diag + jnp.concatenate(parts, axis=0)

    def finish(s, rows):
        lanes = s["lanes"]
        o = _rms(s["o"], gain_ref[:, lanes]) * _silu(g_ref[rows, lanes].astype(F32))
        o_ref[rows, lanes] = o.astype(o_ref.dtype)

    def chunk(ci, carry):
        work = []
        for slot in range(chunks_per_iter):
            rows = pl.ds(pl.multiple_of((ci * chunks_per_iter + slot) * C, C), C)
            work += [(hh, slot, rows, gates(hh, rows)) for hh in range(heads)]
        for hh, slot, _, s in work:
            scores(hh, slot, s)
        for hh, _, _, s in work:
            values(hh, s)
        for _, _, rows, s in work:
            finish(s, rows)
        return carry

    lax.fori_loop(0, n_chunks // chunks_per_iter, chunk, 0)


def hgrn2(proj, lower_bound, out_gain, *, batch, seq, tt=512, heads=HGRN_HEADS_PER_STEP,
          chunks_per_iter=1):
    T = proj.shape[0]
    H = N_HGRN_HEADS // heads
    width = heads * HEAD_DIM
    tt = min(tt, seq)
    nt = seq // tt
    assert tt % (HGRN_CHUNK * chunks_per_iter) == 0

    def sec(k):
        return pl.BlockSpec((tt, width), lambda b, h, c, k=k: (b * nt + c, k * H + h))

    vec = pl.BlockSpec((1, width), lambda b, h, c: (0, h))
    per_head = pltpu.VMEM((chunks_per_iter * heads, HGRN_CHUNK, HEAD_DIM), F32)
    return pl.pallas_call(
        functools.partial(_hgrn_kernel, n_chunks=tt // HGRN_CHUNK, heads=heads,
                          chunks_per_iter=chunks_per_iter),
        grid=(batch, H, nt),
        in_specs=[sec(0), sec(1), sec(2), sec(3), vec, vec],
        out_specs=pl.BlockSpec((tt, width), lambda b, h, c: (b * nt + c, h)),
        out_shape=jax.ShapeDtypeStruct((T, HGRN_WIDTH), BF16),
        scratch_shapes=[pltpu.VMEM((heads, HEAD_DIM, HEAD_DIM), F32), per_head,
                        pltpu.VMEM((HGRN_CHUNK, HGRN_CHUNK), BF16),
                        pltpu.VMEM((HGRN_SUB * HEAD_DIM, HEAD_DIM), BF16),
                        pltpu.VMEM((HGRN_SUB, HGRN_SUB, HEAD_DIM), F32)],
        compiler_params=_params(("parallel", "parallel", "arbitrary")),
        name="hgrn2",
    )(proj, proj, proj, proj,
      lower_bound.reshape(1, HGRN_WIDTH).astype(F32), out_gain.reshape(1, HGRN_WIDTH).astype(F32))


def _attn_kernel(q_ref, k_ref, v_ref, cos_ref, sin_ref, o_ref,
                 qn, kn, vn, qm, km, vm, o1, o2, o3, l1, l2, l3, bias_ref, *, seq, n_par):
    BLK = 128
    ROWS = 512
    NC = ATTN_CLASSES
    per_class = seq // NC
    scale = HEAD_DIM ** -0.5 * math.log2(math.e)

    def rope(i, carry):
        rows = pl.ds(pl.multiple_of(i * ROWS, ROWS), ROWS)
        c = cos_ref[rows, :]
        s = sin_ref[rows, :]
        q = q_ref[rows, :].astype(F32)
        k = k_ref[rows, :].astype(F32)
        qn[rows, :] = (q * c + pltpu.roll(q, HEAD_DIM // 2, 1) * s) * scale
        padded = pl.ds(pl.multiple_of(i * ROWS, ROWS) + ATTN_PAD_N, ROWS)
        kn[padded, :] = k * c + pltpu.roll(k, HEAD_DIM // 2, 1) * s
        vn[padded, :] = v_ref[rows, :].astype(F32)
        return carry

    lax.fori_loop(0, seq // ROWS, rope, 0)

    kn[0:ATTN_PAD_N, :] = jnp.zeros((ATTN_PAD_N, HEAD_DIM), F32)
    vn[0:ATTN_PAD_N, :] = jnp.zeros((ATTN_PAD_N, HEAD_DIM), F32)
    for c in range(NC):
        base = c * (ATTN_PAD_M + per_class)
        km[base:base + ATTN_PAD_M, :] = jnp.zeros((ATTN_PAD_M, HEAD_DIM), F32)
        vm[base:base + ATTN_PAD_M, :] = jnp.zeros((ATTN_PAD_M, HEAD_DIM), F32)

    def regroup(i, carry):
        j0 = pl.multiple_of(i * ROWS, ROWS)
        for c in range(NC):
            src = pl.ds(NC * j0 + c, ROWS, stride=NC)
            qm[pl.ds(c * per_class + j0, ROWS), :] = qn[src, :]
            dst = pl.ds(c * (ATTN_PAD_M + per_class) + ATTN_PAD_M + j0, ROWS)
            km[dst, :] = kn[pl.ds(NC * j0 + c + ATTN_PAD_N, ROWS, stride=NC), :]
            vm[dst, :] = vn[pl.ds(NC * j0 + c + ATTN_PAD_N, ROWS, stride=NC), :]
        return carry

    lax.fori_loop(0, per_class // ROWS, regroup, 0)

    qi = lax.broadcasted_iota(jnp.int32, (BLK, 2 * BLK), 0)
    kj = lax.broadcasted_iota(jnp.int32, (BLK, 2 * BLK), 1)
    band = (kj >= qi) & (kj <= qi + BLK)
    bias_ref[0] = jnp.where(band & (kj >= BLK), 0.0, -jnp.inf)
    bias_ref[1] = jnp.where(band, 0.0, -jnp.inf)
    ones = jnp.ones((2 * BLK, HEAD_DIM), BF16)

    def ds(start, size, stride):
        return pl.ds(start, size) if stride == 1 else pl.ds(start, size, stride=stride)

    def unit_rows(d, u):
        if d == 1:
            return u, pl.ds(u * BLK, BLK), pl.ds(u * BLK + (ATTN_PAD_N - BLK), 2 * BLK), pl.ds(u * BLK, BLK)
        step = d // NC
        nb = per_class // (BLK * step)
        n = u & (nb - 1)
        r = lax.shift_right_logical(u, nb.bit_length() - 1) & (step - 1)
        c = lax.shift_right_logical(u, (nb * step).bit_length() - 1)
        j0 = n * (BLK * step) + r
        return (n, ds(c * per_class + j0, BLK, step),
                ds(c * (ATTN_PAD_M + per_class) + (ATTN_PAD_M - BLK * step) + j0, 2 * BLK, step),
                pl.ds(NC * j0 + c, BLK, stride=d))

    def qk(d, u):
        q_src, k_src, v_src = (qn, kn, vn) if d == 1 else (qm, km, vm)
        n, q_rows, k_rows, out_rows = unit_rows(d, u)
        q = q_src[q_rows, :].astype(BF16)
        k = k_src[k_rows, :].astype(BF16)
        v = v_src[k_rows, :].astype(BF16)
        s = lax.dot_general(q, k, _NT, preferred_element_type=F32)
        return dict(n=n, out_rows=out_rows, s=s, v=v)

    def pv(t):
        s = t["s"] + bias_ref[jnp.minimum(t["n"], 1)]
        t["m"] = jnp.max(s, axis=-1, keepdims=True)
        p = jnp.exp2(s - t["m"]).astype(BF16)
        t["acc"] = jnp.dot(p, jnp.concatenate([t["v"], ones], axis=1), preferred_element_type=F32)

    def store(t, o_out, l_out):
        den = t["acc"][:, HEAD_DIM:]
        o_out[t["out_rows"], :] = t["acc"][:, :HEAD_DIM] / den
        l_out[t["out_rows"], :] = t["m"] + jnp.log2(den)

    for (window, d), o_out, l_out in zip(DILATED_PATTERNS, (o1, o2, o3), (l1, l2, l3)):
        assert window == BLK * d and (d == 1 or d % NC == 0)
        assert BLK * d <= (ATTN_PAD_N if d == 1 else ATTN_PAD_M * NC) and per_class % (BLK * max(d // NC, 1)) == 0
        n_units = seq // BLK
        assert n_units % n_par == 0

        def units(it, carry, d=d, o_out=o_out, l_out=l_out):
            ts = [qk(d, it * n_par + j) for j in range(n_par)]
            for t in ts:
                pv(t)
            for t in ts:
                store(t, o_out, l_out)
            return carry

        lax.fori_loop(0, n_units // n_par, units, 0)

    def combine(i, carry):
        rows = pl.ds(pl.multiple_of(i * ROWS, ROWS), ROWS)
        a1, a2, a3 = l1[rows, :], l2[rows, :], l3[rows, :]
        m = jnp.maximum(jnp.maximum(a1, a2), a3)
        e1, e2, e3 = jnp.exp2(a1 - m), jnp.exp2(a2 - m), jnp.exp2(a3 - m)
        out = (e1 * o1[rows, :] + e2 * o2[rows, :] + e3 * o3[rows, :]) / (e1 + e2 + e3)
        o_ref[rows, :] = out.astype(o_ref.dtype)
        return carry

    lax.fori_loop(0, seq // ROWS, combine, 0)


def _rope_tables(seq):
    half = HEAD_DIM // 2
    inv_freq = ROPE_THETA ** (-jnp.arange(half, dtype=F32) / half)
    ang = jnp.arange(seq, dtype=F32)[:, None] * inv_freq[None, :]
    cos, sin = jnp.cos(ang), jnp.sin(ang)
    return jnp.concatenate([cos, cos], axis=-1), jnp.concatenate([-sin, sin], axis=-1)


def dilated_attention(proj, *, batch, seq, col0, n_par=ATTN_UNITS):
    T = proj.shape[0]
    H = N_ATTN_HEADS
    c0 = col0 // HEAD_DIM
    cos, sin = _rope_tables(seq)

    def sec(k):
        return pl.BlockSpec((seq, HEAD_DIM), lambda b, h, k=k: (b, c0 + k * H + h))

    table = pl.BlockSpec((seq, HEAD_DIM), lambda b, h: (0, 0), pipeline_mode=pl.Buffered(1))
    slab = pltpu.VMEM((seq, HEAD_DIM), F32)
    padded_n = pltpu.VMEM((ATTN_PAD_N + seq, HEAD_DIM), F32)
    padded_m = pltpu.VMEM((ATTN_CLASSES * ATTN_PAD_M + seq, HEAD_DIM), F32)
    return pl.pallas_call(
        functools.partial(_attn_kernel, seq=seq, n_par=n_par),
        grid=(batch, H),
        in_specs=[sec(0), sec(1), sec(2), table, table],
        out_specs=pl.BlockSpec((seq, HEAD_DIM), lambda b, h: (b, h)),
        out_shape=jax.ShapeDtypeStruct((T, ATTN_WIDTH), BF16),
        scratch_shapes=[slab, padded_n, padded_n, slab, padded_m, padded_m] + [slab] * 6
                       + [pltpu.VMEM((2, HEAD_DIM, 2 * HEAD_DIM), F32)],
        compiler_params=_params(("parallel", "parallel")),
        name="dilated_attn",
    )(proj, proj, proj, cos, sin)


def _out_proj_kernel(*refs, route, sub_rows):
    if route:
        oh_ref, oa_ref, x_ref, w_ref, g_ref, wr_ref, xo_ref, h_ref, gate_ref, idx_ref = refs
    else:
        oh_ref, oa_ref, x_ref, w_ref, g_ref, xo_ref, h_ref = refs
    hw = oh_ref.shape[1]
    tm = x_ref.shape[0]
    sub = min(tm, sub_rows)
    for r0 in range(0, tm, sub):
        rows = slice(r0, r0 + sub)
        y = (jnp.dot(oh_ref[rows, :], w_ref[:hw, :], preferred_element_type=F32)
             + jnp.dot(oa_ref[rows, :], w_ref[hw:, :], preferred_element_type=F32))
        x = x_ref[rows, :] + y
        xo_ref[rows, :] = x
        h = _rms(x, g_ref[...])
        h_ref[rows, :] = h.astype(h_ref.dtype)
        if route:
            h_hi = h.astype(BF16)
            h_lo = (h - h_hi.astype(F32)).astype(BF16)
            logits = (jnp.dot(h_hi, wr_ref[0], preferred_element_type=F32)
                      + jnp.dot(h_lo, wr_ref[0], preferred_element_type=F32)
                      + jnp.dot(h_hi, wr_ref[1], preferred_element_type=F32))
            lane = lax.broadcasted_iota(jnp.int32, logits.shape, 1)
            lg = jnp.where(lane < N_EXPERTS, logits, -jnp.inf)
            m1 = jnp.max(lg, axis=-1, keepdims=True)
            i1 = jnp.min(jnp.where(lg == m1, lane, HEAD_DIM), axis=-1, keepdims=True)
            lg = jnp.where(lane == i1, -jnp.inf, lg)
            m2 = jnp.max(lg, axis=-1, keepdims=True)
            i2 = jnp.min(jnp.where(lg == m2, lane, HEAD_DIM), axis=-1, keepdims=True)
            e = jnp.exp(m2 - m1)
            gate_ref[rows, :] = jnp.where(lane == 0, 1.0 / (1.0 + e),
                                          jnp.where(lane == 1, e / (1.0 + e), 0.0))
            idx_ref[rows, :] = jnp.where(lane == 0, i1, jnp.where(lane == 1, i2, 0))


def out_proj_norm(o_hgrn, o_attn, x, w, gain, w_router=None, *, tm=512, sub_rows=OUT_PROJ_SUB_ROWS):
    T, D = x.shape
    tm = min(tm, T)
    route = w_router is not None
    row = lambda width: pl.BlockSpec((tm, width), lambda i: (i, 0))
    whole = lambda a: pl.BlockSpec(a.shape, lambda i: (0, 0), pipeline_mode=pl.Buffered(1))
    gain2 = gain.reshape(1, D).astype(F32)
    args = [o_hgrn, o_attn, x, w, gain2]
    in_specs = [row(o_hgrn.shape[1]), row(o_attn.shape[1]), row(D), whole(w), whole(gain2)]
    out_specs = [row(D), row(D)]
    out_shape = [jax.ShapeDtypeStruct((T, D), F32), jax.ShapeDtypeStruct((T, D), F32 if route else BF16)]
    if route:
        wr = jnp.zeros((D, HEAD_DIM), F32).at[:, :N_EXPERTS].set(w_router.astype(F32))
        wr_hi = wr.astype(BF16)
        wr = jnp.stack([wr_hi, (wr - wr_hi.astype(F32)).astype(BF16)])
        args.append(wr)
        in_specs.append(pl.BlockSpec(wr.shape, lambda i: (0, 0, 0), pipeline_mode=pl.Buffered(1)))
        out_specs += [row(HEAD_DIM), row(HEAD_DIM)]
        out_shape += [jax.ShapeDtypeStruct((T, HEAD_DIM), F32),
                      jax.ShapeDtypeStruct((T, HEAD_DIM), jnp.int32)]
    return pl.pallas_call(
        functools.partial(_out_proj_kernel, route=route, sub_rows=sub_rows),
        grid=(T // tm,),
        in_specs=in_specs,
        out_specs=out_specs,
        out_shape=out_shape,
        compiler_params=_params(("parallel",)),
        name="out_proj_route" if route else "out_proj",
    )(*args)


def _fresh_weights(be_ref, i):
    return (i == 0) | (be_ref[i] != be_ref[jnp.maximum(i - 1, 0)])


def _ffn_up_kernel(be_ref, na_ref, x_ref, wg_ref, wu_ref, a_ref, wg16, wu16):
    i = pl.program_id(1)
    active = i < na_ref[0]

    @pl.when(active & _fresh_weights(be_ref, i))
    def _():
        wg16[...] = wg_ref[...].astype(BF16)
        wu16[...] = wu_ref[...].astype(BF16)

    @pl.when(active)
    def _():
        x = x_ref[...]
        g = jnp.dot(x, wg16[...], preferred_element_type=F32)
        u = jnp.dot(x, wu16[...], preferred_element_type=F32)
        a_ref[...] = (_silu(g) * u).astype(a_ref.dtype)

    @pl.when(jnp.logical_not(active))
    def _():
        a_ref[...] = jnp.zeros_like(a_ref)


def _ffn_down_kernel(*refs, residual):
    if residual:
        be_ref, na_ref, a_ref, wd_ref, x_ref, o_ref, wd16 = refs
    else:
        be_ref, na_ref, a_ref, wd_ref, o_ref, wd16 = refs
    i = pl.program_id(1)
    active = i < na_ref[0]

    @pl.when(active & _fresh_weights(be_ref, i))
    def _():
        wd16[...] = wd_ref[...].astype(BF16)

    @pl.when(active)
    def _():
        y = jnp.dot(a_ref[...], wd16[...], preferred_element_type=F32)
        o_ref[...] = x_ref[...] + y if residual else y

    @pl.when(jnp.logical_not(active))
    def _():
        o_ref[...] = jnp.zeros_like(o_ref)


def _block(i, na):
    return jnp.minimum(i, na[0] - 1)


def ffn_up(rows, block_expert, n_active, w_gate, w_up, *, tm, tf):
    R, D = rows.shape
    F = w_gate.shape[2]
    tf = min(tf, F)
    wspec =pl.BlockSpec((None, D, tf), lambda f, i, be, na: (be[_block(i, na)], 0, f))
    return pl.pallas_call(
        _ffn_up_kernel,
        grid_spec=pltpu.PrefetchScalarGridSpec(
            num_scalar_prefetch=2,
            grid=(F // tf, R // tm),
            in_specs=[pl.BlockSpec((tm, D), lambda f, i, be, na: (_block(i, na), 0)), wspec, wspec],
            out_specs=pl.BlockSpec((tm, tf), lambda f, i, be, na: (i, f)),
            scratch_shapes=[pltpu.VMEM((D, tf), BF16), pltpu.VMEM((D, tf), BF16)]),
        out_shape=jax.ShapeDtypeStruct((R, F), BF16),
        compiler_params=_params(("arbitrary", "arbitrary")),
        name="ffn_up",
    )(block_expert, n_active, rows, w_gate, w_up)


def ffn_down(act, block_expert, n_active, w_down, x=None, *, tm, tn):
    R, F = act.shape
    D = w_down.shape[2]
    tn = min(tn, D)
    residual = x is not None
    tile = pl.BlockSpec((tm, tn), lambda n, i, be, na: (i, n))
    in_specs = [pl.BlockSpec((tm, F), lambda n, i, be, na: (_block(i, na), 0)),
                pl.BlockSpec((None, F, tn), lambda n, i, be, na: (be[_block(i, na)], 0, n))]
    args = [act, w_down]
    if residual:
        in_specs.append(tile)
        args.append(x)
    return pl.pallas_call(
        functools.partial(_ffn_down_kernel, residual=residual),
        grid_spec=pltpu.PrefetchScalarGridSpec(
            num_scalar_prefetch=2,
            grid=(D // tn, R // tm),
            in_specs=in_specs,
            out_specs=tile,
            scratch_shapes=[pltpu.VMEM((F, tn), BF16)]),
        out_shape=jax.ShapeDtypeStruct((R, D), F32),
        compiler_params=_params(("arbitrary", "arbitrary")),
        name="ffn_down",
    )(block_expert, n_active, *args)


def dense_ffn(h, x, w_gate, w_up, w_down, index, *, tm=1024, tf=512, tm_down=512, tn=512):
    T = x.shape[0]

    def blocks(t):
        return jnp.full((T // t,), index, jnp.int32), jnp.full((1,), T // t, jnp.int32)

    act = ffn_up(h, *blocks(tm), w_gate, w_up, tm=tm, tf=tf)
    return ffn_down(act, *blocks(tm_down), w_down, x, tm=tm_down, tn=tn)


def _row_copy(src_hbm, dst_ref, sem, src_row, group, j):
    return pltpu.make_async_copy(src_hbm.at[pl.ds(src_row, 1), :], dst_ref.at[group, pl.ds(j, 1), :], sem)


def _gather_rows_kernel(na_ref, idx_ref, nxt_ref, h_hbm, o_ref, buf, sem):
    tm = buf.shape[1] * SUBLANES
    i = pl.program_id(0)
    slot = i % 2

    def rows(ids_ref, s, start):
        def body(g, c):
            for j in range(SUBLANES):
                cp = _row_copy(h_hbm, buf.at[s], sem.at[s], ids_ref[0, 0, g * SUBLANES + j], g, j)
                if start:
                    cp.start(priority=j % 2)
                else:
                    cp.wait()
            return c

        lax.fori_loop(0, tm // SUBLANES, body, 0)

    @pl.when(i == 0)
    def _():
        rows(idx_ref, 0, True)

    @pl.when(i + 1 < na_ref[0])
    def _():
        rows(nxt_ref, 1 - slot, True)

    @pl.when(i < na_ref[0])
    def _():
        rows(idx_ref, slot, False)
        o_ref[...] = buf[slot].reshape(o_ref.shape).astype(o_ref.dtype)

    @pl.when(i >= na_ref[0])
    def _():
        o_ref[...] = jnp.zeros_like(o_ref)


def gather_rows(h, row_token, n_active, *, tm):
    T, D = h.shape
    R = row_token.shape[0]
    ids = row_token.reshape(R // tm, 1, tm)
    return pl.pallas_call(
        _gather_rows_kernel,
        grid_spec=pltpu.PrefetchScalarGridSpec(
            num_scalar_prefetch=1,
            grid=(R // tm,),
            in_specs=[pl.BlockSpec((1, 1, tm), lambda i, na: (_block(i, na), 0, 0),
                                   memory_space=pltpu.SMEM),
                      pl.BlockSpec((1, 1, tm), lambda i, na: (_block(i + 1, na), 0, 0),
                                   memory_space=pltpu.SMEM),
                      pl.BlockSpec(memory_space=pl.ANY)],
            out_specs=pl.BlockSpec((tm, D), lambda i, na: (i, 0)),
            scratch_shapes=[pltpu.VMEM((2, tm // SUBLANES, SUBLANES, D), F32),
                            pltpu.SemaphoreType.DMA((2,))]),
        out_shape=jax.ShapeDtypeStruct((R, D), BF16),
        compiler_params=_params(("arbitrary",)),
        name="gather_rows",
    )(n_active, ids, ids, h)


def _combine_kernel(*refs, final):
    if final:
        dest_ref, x_ref, gate_ref, gain_ref, y_hbm, o_ref, buf0, buf1, sem = refs
    else:
        dest_ref, x_ref, gate_ref, y_hbm, o_ref, buf0, buf1, sem = refs
    tm = buf0.shape[0] * SUBLANES

    def copies(g, j):
        r = g * SUBLANES + j
        return (_row_copy(y_hbm, buf0, sem.at[0], dest_ref[0, 0, 2 * r], g, j),
                _row_copy(y_hbm, buf1, sem.at[1], dest_ref[0, 0, 2 * r + 1], g, j))

    def start(g, c):
        for j in range(SUBLANES):
            first, second = copies(g, j)
            first.start(priority=0)
            second.start(priority=1)
        return c

    def wait(g, c):
        for j in range(SUBLANES):
            for cp in copies(g, j):
                cp.wait()
        return c

    lax.fori_loop(0, tm // SUBLANES, start, 0)
    lax.fori_loop(0, tm // SUBLANES, wait, 0)
    gates = gate_ref[...]
    x = (x_ref[...] + gates[:, 0:1] * buf0[...].reshape(x_ref.shape)
         + gates[:, 1:2] * buf1[...].reshape(x_ref.shape))
    o_ref[...] = _rms(x, gain_ref[...]) if final else x


def combine_rows(x, y_rows, dest, gates, final_gain=None, *, tm=256):
    T, D = x.shape
    final = final_gain is not None
    row = pl.BlockSpec((tm, D), lambda i: (i, 0))
    in_specs = [pl.BlockSpec((1, 1, 2 * tm), lambda i: (i, 0, 0), memory_space=pltpu.SMEM),
                row, pl.BlockSpec((tm, gates.shape[1]), lambda i: (i, 0))]
    args = [dest.reshape(T // tm, 1, 2 * tm), x, gates]
    if final:
        in_specs.append(pl.BlockSpec((1, D), lambda i: (0, 0)))
        args.append(final_gain.reshape(1, D).astype(F32))
    in_specs.append(pl.BlockSpec(memory_space=pl.ANY))
    args.append(y_rows)
    return pl.pallas_call(
        functools.partial(_combine_kernel, final=final),
        grid=(T // tm,),
        in_specs=in_specs,
        out_specs=row,
        out_shape=jax.ShapeDtypeStruct((T, D), F32),
        scratch_shapes=[pltpu.VMEM((tm // SUBLANES, SUBLANES, D), F32),
                        pltpu.VMEM((tm // SUBLANES, SUBLANES, D), F32),
                        pltpu.SemaphoreType.DMA((2,))],
        compiler_params=_params(("arbitrary",)),
        name="combine_rows",
    )(*args)


def _rms_rows_kernel(x_ref, g_ref, o_ref):
    o_ref[...] = _rms(x_ref[...], g_ref[...])


def rms_rows(x, gain, *, tm=512):
    T, D = x.shape
    tm = min(tm, T)
    return pl.pallas_call(
        _rms_rows_kernel,
        grid=(T // tm,),
        in_specs=[pl.BlockSpec((tm, D), lambda i: (i, 0)), pl.BlockSpec((1, D), lambda i: (0, 0))],
        out_specs=pl.BlockSpec((tm, D), lambda i: (i, 0)),
        out_shape=jax.ShapeDtypeStruct((T, D), F32),
        compiler_params=_params(("parallel",)),
        name="final_norm",
    )(x, gain.reshape(1, D).astype(F32))


def moe_layer(h, x, gates, top_idx, w_gate, w_up, w_down, index, final_gain=None, *,
              tm=512, tm_down=512, combine_tm=256):
    T, D = x.shape
    E = w_gate.shape[1]
    w_gate, w_up, w_down = (w.reshape((-1,) + w.shape[2:]) for w in (w_gate, w_up, w_down))
    n_assign = 2 * T
    flat_expert = top_idx[:, :2].reshape(n_assign)
    onehot = (flat_expert[:, None] == jnp.arange(E, dtype=jnp.int32)[None, :]).astype(jnp.int32)
    csum = jnp.cumsum(onehot, axis=0)
    rank = jnp.take_along_axis(csum, flat_expert[:, None], axis=1)[:, 0] - 1
    counts = csum[-1]
    padded = (counts + tm - 1) // tm * tm
    padded_ends = jnp.cumsum(padded)
    dest = (padded_ends - padded)[flat_expert] + rank
    n_blocks = -(-n_assign // tm) + E
    row_token = jnp.zeros((n_blocks * tm,), jnp.int32).at[dest].set(
        jnp.arange(n_assign, dtype=jnp.int32) // 2)
    block_expert = jnp.minimum(
        jnp.searchsorted(padded_ends, jnp.arange(n_blocks, dtype=jnp.int32) * tm, side='right'),
        E - 1).astype(jnp.int32) + index * E
    n_active = (padded_ends[-1:] // tm).astype(jnp.int32)
    rows = gather_rows(h, row_token, n_active, tm=tm)
    act = ffn_up(rows, block_expert, n_active, w_gate, w_up, tm=tm, tf=1024)
    sub = tm // tm_down
    y_rows = ffn_down(act, jnp.repeat(block_expert, sub), n_active * sub, w_down, tm=tm_down, tn=512)
    return combine_rows(x, y_rows, dest.astype(jnp.int32), gates, final_gain, tm=combine_tm)


def kernel(x, norm_mix, w_in, lb_logits, hgrn_norm, w_out, norm_ffn, dense_w_gate, dense_w_up,
           dense_w_down, moe_router, moe_w_gate, moe_w_up, moe_w_down, final_norm):
    B, S, D = x.shape
    depth = w_in.shape[0]
    lb_sm = jax.nn.softmax(lb_logits.astype(F32), axis=0)
    lower_bounds = jnp.cumsum(lb_sm, axis=0) - lb_sm[0:1]
    xt = x.reshape(B * S, D)
    for layer in range(depth):
        cfg = LAYER_TILING[layer % len(LAYER_TILING)]
        proj = norm_matmul(xt, norm_mix[layer], w_in, layer)
        o_hgrn = hgrn2(proj, lower_bounds[layer], hgrn_norm[layer], batch=B, seq=S, **cfg["hgrn"])
        o_attn = dilated_attention(proj, batch=B, seq=S, col0=4 * HGRN_WIDTH)
        i = layer // 2
        last = layer == depth - 1
        if layer % 2 == 0:
            xt, h = out_proj_norm(o_hgrn, o_attn, xt, w_out[layer].astype(BF16), norm_ffn[layer],
                                  **cfg["out_proj"])
            xt = dense_ffn(h, xt, dense_w_gate, dense_w_up, dense_w_down, i, **cfg["ffn"])
            if last:
                xt = rms_rows(xt, final_norm)
        else:
            xt, h, gates, idx = out_proj_norm(o_hgrn, o_attn, xt, w_out[layer].astype(BF16),
                                              norm_ffn[layer], moe_router[i], **cfg["out_proj"])
            xt = moe_layer(h, xt, gates, idx, moe_w_gate, moe_w_up, moe_w_down, i,
                           final_norm if last else None, **cfg["ffn"])
    return xt.reshape(B, S, D)
```

```python
import functools
import math

import jax
import jax.numpy as jnp
from jax import lax
from jax.experimental import pallas as pl
from jax.experimental.pallas import tpu as pltpu

F32 = jnp.float32
BF16 = jnp.bfloat16

HEAD_DIM = 128
N_HGRN_HEADS = 8
N_ATTN_HEADS = 8
HGRN_WIDTH = N_HGRN_HEADS * HEAD_DIM
ATTN_WIDTH = N_ATTN_HEADS * HEAD_DIM
HGRN_CHUNK = 64
HGRN_SUB = 8
HGRN_CHUNKS_PER_ITER = 4
HGRN_HEADS_PER_STEP = 8
DILATED_PATTERNS = ((128, 1), (512, 4), (2048, 16))
ATTN_CLASSES = 4
ATTN_PAD_N = 128
ATTN_PAD_M = 512
ATTN_UNITS = 32
SUBLANES = 8
OUT_PROJ_SUB_ROWS = 256
ROPE_THETA = 10000.0
N_EXPERTS = 8
NORM_EPS = 1e-6
VMEM_LIMIT = 60 * 1024 * 1024

_NT = (((1,), (1,)), ((), ()))
_TN = (((0,), (0,)), ((), ()))


def _params(semantics):
    return pltpu.CompilerParams(dimension_semantics=semantics, vmem_limit_bytes=VMEM_LIMIT)


def _silu(z):
    return z * jax.nn.sigmoid(z)


def _rms(z, gain):
    ms = jnp.mean(z * z, axis=-1, keepdims=True)
    return z * lax.rsqrt(ms + NORM_EPS) * gain


def _in_proj_kernel(x_ref, g_ref, w_ref, o_ref, h_ref):
    @pl.when(pl.program_id(1) == 0)
    def _():
        h_ref[...] = _rms(x_ref[...], g_ref[...]).astype(BF16)

    o_ref[...] = jnp.dot(h_ref[...], w_ref[...].astype(BF16),
                         preferred_element_type=F32).astype(o_ref.dtype)


def norm_matmul(x, gain, w, index, *, tm=1024, tn=1024):
    T, D = x.shape
    N = w.shape[2]
    tm, tn = min(tm, T), min(tn, N)
    return pl.pallas_call(
        _in_proj_kernel,
        grid=(T // tm, N // tn),
        in_specs=[pl.BlockSpec((tm, D), lambda i, j: (i, 0)),
                  pl.BlockSpec((1, D), lambda i, j: (0, 0)),
                  pl.BlockSpec((None, D, tn), lambda i, j: (index, 0, j))],
        out_specs=pl.BlockSpec((tm, tn), lambda i, j: (i, j)),
        out_shape=jax.ShapeDtypeStruct((T, N), BF16),
        scratch_shapes=[pltpu.VMEM((tm, D), BF16)],
        compiler_params=_params(("parallel", "arbitrary")),
        name="in_proj",
    )(x, gain.reshape(1, D).astype(F32), w)


def _hgrn_kernel(q_ref, f_ref, i_ref, g_ref, lb_ref, gain_ref, o_ref,
                 st_ref, k_ref, tril_ref, select_ref, causal_ref, *, n_chunks, heads, chunks_per_iter):
    C, SUB = HGRN_CHUNK, HGRN_SUB

    @pl.when(pl.program_id(2) == 0)
    def _():
        st_ref[...] = jnp.zeros_like(st_ref)
        row = lax.broadcasted_iota(jnp.int32, (C, C), 0)
        col = lax.broadcasted_iota(jnp.int32, (C, C), 1)
        tril_ref[...] = (col <= row).astype(BF16)
        group = lax.shift_right_logical(
            lax.broadcasted_iota(jnp.int32, (SUB * HEAD_DIM, HEAD_DIM), 0), HEAD_DIM.bit_length() - 1)
        select_ref[...] = (group == lax.broadcasted_iota(jnp.int32, (SUB * HEAD_DIM, HEAD_DIM), 1)
                           ).astype(BF16)
        sub_row = lax.broadcasted_iota(jnp.int32, (SUB, SUB, HEAD_DIM), 1)
        key = lax.broadcasted_iota(jnp.int32, (SUB, SUB, HEAD_DIM), 0)
        causal_ref[...] = jnp.where(sub_row >= key, 0.0, -jnp.inf)

    pairs = []
    span = C
    while span > SUB:
        half = span // 2
        pairs += [(base + half, base + span, base, base + half) for base in range(0, C, span)]
        span = half

    def gates(hh, rows):
        lanes = slice(hh * HEAD_DIM, (hh + 1) * HEAD_DIM)
        lb = lb_ref[:, lanes]
        f = lb + (1.0 - lb) * jax.nn.sigmoid(f_ref[rows, lanes].astype(F32))
        g = jnp.log2(f)
        g_hi = g.astype(BF16)
        g_lo = (g - g_hi.astype(F32)).astype(BF16)
        tril = tril_ref[...]
        b = (jnp.dot(tril, g_hi, preferred_element_type=F32)
             + jnp.dot(tril, g_lo, preferred_element_type=F32))
        return dict(lanes=lanes, b=b, kk=1.0 - f, q=_silu(q_ref[rows, lanes].astype(F32)),
                    v=i_ref[rows, lanes].astype(F32))

    def scores(hh, slot, s):
        b, kk, q, v = s["b"], s["kk"], s["q"], s["v"]
        k_scr = k_ref.at[slot * heads + hh]
        k_scr[...] = jnp.log2(kk) - b
        s["v16"] = v.astype(BF16)
        b_end = b[C - 1:C, :]
        st = st_ref[hh]
        s["o"] = lax.dot_general((q * jnp.exp2(b)).astype(BF16), st.astype(BF16), _NT,
                                 preferred_element_type=F32)
        kd = (kk * jnp.exp2(b_end - b)).astype(BF16)
        st_ref[hh] = st * jnp.exp2(b_end) + lax.dot_general(s["v16"], kd, _TN,
                                                            preferred_element_type=F32)
        s["a_off"] = []
        for t0, t1, s0, s1 in pairs:
            b_mid = b[s1 - 1:s1, :]
            qa = (q[t0:t1] * jnp.exp2(b[t0:t1] - b_mid)).astype(BF16)
            ka = (kk[s0:s1] * jnp.exp2(b_mid - b[s0:s1])).astype(BF16)
            s["a_off"].append(lax.dot_general(qa, ka, _NT, preferred_element_type=F32))
        blocks = []
        for r in range(0, C, SUB):
            qs = q[r:r + SUB]
            bs = b[r:r + SUB]
            groups = []
            for j in range(SUB):
                d = bs + jnp.broadcast_to(k_scr[r + j:r + j + 1, :], (SUB, HEAD_DIM))
                if j:
                    d = d + causal_ref[j]
                groups.append(qs * jnp.exp2(d))
            blocks.append(jnp.concatenate(groups, axis=1))
        a = jnp.dot(jnp.concatenate(blocks, axis=0).astype(BF16), select_ref[...],
                    preferred_element_type=F32)
        s["a_diag"] = jnp.concatenate(
            [a[r:r + SUB] if r == 0 else pltpu.roll(a[r:r + SUB], r, 1) for r in range(0, C, SUB)], axis=0)

    def values(hh, s):
        parts = [jnp.zeros((SUB, HEAD_DIM), F32)] + [None] * (C // SUB - 1)

        def add(idx, piece):
            parts[idx] = piece if parts[idx] is None else parts[idx] + piece

        for (t0, t1, s0, s1), a in zip(pairs, s["a_off"]):
            blk = jnp.dot(a.astype(BF16), s["v16"][s0:s1], preferred_element_type=F32)
            for j in range((t1 - t0) // SUB):
                add(t0 // SUB + j, blk[j * SUB:(j + 1) * SUB])
        v_rows = jnp.concatenate([s["v16"], jnp.zeros((HEAD_DIM - C, HEAD_DIM), BF16)], axis=0)
        diag = jnp.dot(s["a_diag"].astype(BF16), v_rows, preferred_element_type=F32)
        s["o"] = s["o"] + diag + jnp.concatenate(parts, axis=0)

    def finish(s, rows):
        lanes = s["lanes"]
        o = _rms(s["o"], gain_ref[:, lanes]) * _silu(g_ref[rows, lanes].astype(F32))
        o_ref[rows, lanes] = o.astype(o_ref.dtype)

    def chunk(ci, carry):
        work = []
        for slot in range(chunks_per_iter):
            rows = pl.ds(pl.multiple_of((ci * chunks_per_iter + slot) * C, C), C)
            work += [(hh, slot, rows, gates(hh, rows)) for hh in range(heads)]
        for hh, slot, _, s in work:
            scores(hh, slot, s)
        for hh, _, _, s in work:
            values(hh, s)
        for _, _, rows, s in work:
            finish(s, rows)
        return carry

    lax.fori_loop(0, n_chunks // chunks_per_iter, chunk, 0)


def hgrn2(proj, lower_bound, out_gain, *, batch, seq, tt=512, heads=HGRN_HEADS_PER_STEP,
          chunks_per_iter=HGRN_CHUNKS_PER_ITER):
    T = proj.shape[0]
    H = N_HGRN_HEADS // heads
    width = heads * HEAD_DIM
    tt = min(tt, seq)
    nt = seq // tt
    assert tt % (HGRN_CHUNK * chunks_per_iter) == 0

    def sec(k):
        return pl.BlockSpec((tt, width), lambda b, h, c, k=k: (b * nt + c, k * H + h))

    vec = pl.BlockSpec((1, width), lambda b, h, c: (0, h))
    per_head = pltpu.VMEM((chunks_per_iter * heads, HGRN_CHUNK, HEAD_DIM), F32)
    return pl.pallas_call(
        functools.partial(_hgrn_kernel, n_chunks=tt // HGRN_CHUNK, heads=heads,
                          chunks_per_iter=chunks_per_iter),
        grid=(batch, H, nt),
        in_specs=[sec(0), sec(1), sec(2), sec(3), vec, vec],
        out_specs=pl.BlockSpec((tt, width), lambda b, h, c: (b * nt + c, h)),
        out_shape=jax.ShapeDtypeStruct((T, HGRN_WIDTH), BF16),
        scratch_shapes=[pltpu.VMEM((heads, HEAD_DIM, HEAD_DIM), F32), per_head,
                        pltpu.VMEM((HGRN_CHUNK, HGRN_CHUNK), BF16),
                        pltpu.VMEM((HGRN_SUB * HEAD_DIM, HEAD_DIM), BF16),
                        pltpu.VMEM((HGRN_SUB, HGRN_SUB, HEAD_DIM), F32)],
        compiler_params=_params(("parallel", "parallel", "arbitrary")),
        name="hgrn2",
    )(proj, proj, proj, proj,
      lower_bound.reshape(1, HGRN_WIDTH).astype(F32), out_gain.reshape(1, HGRN_WIDTH).astype(F32))


def _attn_kernel(q_ref, k_ref, v_ref, cos_ref, sin_ref, o_ref,
                 qn, kn, vn, qm, km, vm, o1, o2, o3, l1, l2, l3, bias_ref, *, seq, n_par):
    BLK = 128
    ROWS = 512
    NC = ATTN_CLASSES
    per_class = seq // NC
    scale = HEAD_DIM ** -0.5 * math.log2(math.e)

    def rope(i, carry):
        rows = pl.ds(pl.multiple_of(i * ROWS, ROWS), ROWS)
        c = cos_ref[rows, :]
        s = sin_ref[rows, :]
        q = q_ref[rows, :].astype(F32)
        k = k_ref[rows, :].astype(F32)
        qn[rows, :] = (q * c + pltpu.roll(q, HEAD_DIM // 2, 1) * s) * scale
        padded = pl.ds(pl.multiple_of(i * ROWS, ROWS) + ATTN_PAD_N, ROWS)
        kn[padded, :] = k * c + pltpu.roll(k, HEAD_DIM // 2, 1) * s
        vn[padded, :] = v_ref[rows, :].astype(F32)
        return carry

    lax.fori_loop(0, seq // ROWS, rope, 0)

    kn[0:ATTN_PAD_N, :] = jnp.zeros((ATTN_PAD_N, HEAD_DIM), F32)
    vn[0:ATTN_PAD_N, :] = jnp.zeros((ATTN_PAD_N, HEAD_DIM), F32)
    for c in range(NC):
        base = c * (ATTN_PAD_M + per_class)
        km[base:base + ATTN_PAD_M, :] = jnp.zeros((ATTN_PAD_M, HEAD_DIM), F32)
        vm[base:base + ATTN_PAD_M, :] = jnp.zeros((ATTN_PAD_M, HEAD_DIM), F32)

    def regroup(i, carry):
        j0 = pl.multiple_of(i * ROWS, ROWS)
        for c in range(NC):
            src = pl.ds(NC * j0 + c, ROWS, stride=NC)
            qm[pl.ds(c * per_class + j0, ROWS), :] = qn[src, :]
            dst = pl.ds(c * (ATTN_PAD_M + per_class) + ATTN_PAD_M + j0, ROWS)
            km[dst, :] = kn[pl.ds(NC * j0 + c + ATTN_PAD_N, ROWS, stride=NC), :]
            vm[dst, :] = vn[pl.ds(NC * j0 + c + ATTN_PAD_N, ROWS, stride=NC), :]
        return carry

    lax.fori_loop(0, per_class // ROWS, regroup, 0)

    qi = lax.broadcasted_iota(jnp.int32, (BLK, 2 * BLK), 0)
    kj = lax.broadcasted_iota(jnp.int32, (BLK, 2 * BLK), 1)
    band = (kj >= qi) & (kj <= qi + BLK)
    bias_ref[0] = jnp.where(band & (kj >= BLK), 0.0, -jnp.inf)
    bias_ref[1] = jnp.where(band, 0.0, -jnp.inf)
    ones = jnp.ones((2 * BLK, HEAD_DIM), BF16)

    def ds(start, size, stride):
        return pl.ds(start, size) if stride == 1 else pl.ds(start, size, stride=stride)

    def unit_rows(d, u):
        if d == 1:
            return u, pl.ds(u * BLK, BLK), pl.ds(u * BLK + (ATTN_PAD_N - BLK), 2 * BLK), pl.ds(u * BLK, BLK)
        step = d // NC
        nb = per_class // (BLK * step)
        n = u & (nb - 1)
        r = lax.shift_right_logical(u, nb.bit_length() - 1) & (step - 1)
        c = lax.shift_right_logical(u, (nb * step).bit_length() - 1)
        j0 = n * (BLK * step) + r
        return (n, ds(c * per_class + j0, BLK, step),
                ds(c * (ATTN_PAD_M + per_class) + (ATTN_PAD_M - BLK * step) + j0, 2 * BLK, step),
                pl.ds(NC * j0 + c, BLK, stride=d))

    def qk(d, u):
        q_src, k_src, v_src = (qn, kn, vn) if d == 1 else (qm, km, vm)
        n, q_rows, k_rows, out_rows = unit_rows(d, u)
        q = q_src[q_rows, :].astype(BF16)
        k = k_src[k_rows, :].astype(BF16)
        v = v_src[k_rows, :].astype(BF16)
        s = lax.dot_general(q, k, _NT, preferred_element_type=F32)
        return dict(n=n, out_rows=out_rows, s=s, v=v)

    def pv(t):
        s = t["s"] + bias_ref[jnp.minimum(t["n"], 1)]
        t["m"] = jnp.max(s, axis=-1, keepdims=True)
        p = jnp.exp2(s - t["m"]).astype(BF16)
        t["acc"] = jnp.dot(p, jnp.concatenate([t["v"], ones], axis=1), preferred_element_type=F32)

    def store(t, o_out, l_out):
        den = t["acc"][:, HEAD_DIM:]
        o_out[t["out_rows"], :] = t["acc"][:, :HEAD_DIM] / den
        l_out[t["out_rows"], :] = t["m"] + jnp.log2(den)

    for (window, d), o_out, l_out in zip(DILATED_PATTERNS, (o1, o2, o3), (l1, l2, l3)):
        assert window == BLK * d and (d == 1 or d % NC == 0)
        assert BLK * d <= (ATTN_PAD_N if d == 1 else ATTN_PAD_M * NC) and per_class % (BLK * max(d // NC, 1)) == 0
        n_units = seq // BLK
        assert n_units % n_par == 0

        def units(it, carry, d=d, o_out=o_out, l_out=l_out):
            ts = [qk(d, it * n_par + j) for j in range(n_par)]
            for t in ts:
                pv(t)
            for t in ts:
                store(t, o_out, l_out)
            return carry

        lax.fori_loop(0, n_units // n_par, units, 0)

    def combine(i, carry):
        rows = pl.ds(pl.multiple_of(i * ROWS, ROWS), ROWS)
        a1, a2, a3 = l1[rows, :], l2[rows, :], l3[rows, :]
        m = jnp.maximum(jnp.maximum(a1, a2), a3)
        e1, e2, e3 = jnp.exp2(a1 - m), jnp.exp2(a2 - m), jnp.exp2(a3 - m)
        out = (e1 * o1[rows, :] + e2 * o2[rows, :] + e3 * o3[rows, :]) / (e1 + e2 + e3)
        o_ref[rows, :] = out.astype(o_ref.dtype)
        return carry

    lax.fori_loop(0, seq // ROWS, combine, 0)


def _rope_tables(seq):
    half = HEAD_DIM // 2
    inv_freq = ROPE_THETA ** (-jnp.arange(half, dtype=F32) / half)
    ang = jnp.arange(seq, dtype=F32)[:, None] * inv_freq[None, :]
    cos, sin = jnp.cos(ang), jnp.sin(ang)
    return jnp.concatenate([cos, cos], axis=-1), jnp.concatenate([-sin, sin], axis=-1)


def dilated_attention(proj, *, batch, seq, col0, n_par=ATTN_UNITS):
    T = proj.shape[0]
    H = N_ATTN_HEADS
    c0 = col0 // HEAD_DIM
    cos, sin = _rope_tables(seq)

    def sec(k):
        return pl.BlockSpec((seq, HEAD_DIM), lambda b, h, k=k: (b, c0 + k * H + h))

    table = pl.BlockSpec((seq, HEAD_DIM), lambda b, h: (0, 0), pipeline_mode=pl.Buffered(1))
    slab = pltpu.VMEM((seq, HEAD_DIM), F32)
    padded_n = pltpu.VMEM((ATTN_PAD_N + seq, HEAD_DIM), F32)
    padded_m = pltpu.VMEM((ATTN_CLASSES * ATTN_PAD_M + seq, HEAD_DIM), F32)
    return pl.pallas_call(
        functools.partial(_attn_kernel, seq=seq, n_par=n_par),
        grid=(batch, H),
        in_specs=[sec(0), sec(1), sec(2), table, table],
        out_specs=pl.BlockSpec((seq, HEAD_DIM), lambda b, h: (b, h)),
        out_shape=jax.ShapeDtypeStruct((T, ATTN_WIDTH), BF16),
        scratch_shapes=[slab, padded_n, padded_n, slab, padded_m, padded_m] + [slab] * 6
                       + [pltpu.VMEM((2, HEAD_DIM, 2 * HEAD_DIM), F32)],
        compiler_params=_params(("parallel", "parallel")),
        name="dilated_attn",
    )(proj, proj, proj, cos, sin)


def _out_proj_kernel(*refs, route, sub_rows):
    if route:
        oh_ref, oa_ref, x_ref, w_ref, g_ref, wr_ref, xo_ref, h_ref, gate_ref, idx_ref = refs
    else:
        oh_ref, oa_ref, x_ref, w_ref, g_ref, xo_ref, h_ref = refs
    hw = oh_ref.shape[1]
    tm = x_ref.shape[0]
    sub = min(tm, sub_rows)
    for r0 in range(0, tm, sub):
        rows = slice(r0, r0 + sub)
        y = (jnp.dot(oh_ref[rows, :], w_ref[:hw, :], preferred_element_type=F32)
             + jnp.dot(oa_ref[rows, :], w_ref[hw:, :], preferred_element_type=F32))
        x = x_ref[rows, :] + y
        xo_ref[rows, :] = x
        h = _rms(x, g_ref[...])
        h_ref[rows, :] = h.astype(h_ref.dtype)
        if route:
            h_hi = h.astype(BF16)
            h_lo = (h - h_hi.astype(F32)).astype(BF16)
            logits = (jnp.dot(h_hi, wr_ref[0], preferred_element_type=F32)
                      + jnp.dot(h_lo, wr_ref[0], preferred_element_type=F32)
                      + jnp.dot(h_hi, wr_ref[1], preferred_element_type=F32))
            lane = lax.broadcasted_iota(jnp.int32, logits.shape, 1)
            lg = jnp.where(lane < N_EXPERTS, logits, -jnp.inf)
            m1 = jnp.max(lg, axis=-1, keepdims=True)
            i1 = jnp.min(jnp.where(lg == m1, lane, HEAD_DIM), axis=-1, keepdims=True)
            lg = jnp.where(lane == i1, -jnp.inf, lg)
            m2 = jnp.max(lg, axis=-1, keepdims=True)
            i2 = jnp.min(jnp.where(lg == m2, lane, HEAD_DIM), axis=-1, keepdims=True)
            e = jnp.exp(m2 - m1)
            gate_ref[rows, :] = jnp.where(lane == 0, 1.0 / (1.0 + e),
                                          jnp.where(lane == 1, e / (1.0 + e), 0.0))
            idx_ref[rows, :] = jnp.where(lane == 0, i1, jnp.where(lane == 1, i2, 0))


def out_proj_norm(o_hgrn, o_attn, x, w, gain, w_router=None, *, tm=512, sub_rows=OUT_PROJ_SUB_ROWS):
    T, D = x.shape
    tm = min(tm, T)
    route = w_router is not None
    row = lambda width: pl.BlockSpec((tm, width), lambda i: (i, 0))
    whole = lambda a: pl.BlockSpec(a.shape, lambda i: (0, 0), pipeline_mode=pl.Buffered(1))
    gain2 = gain.reshape(1, D).astype(F32)
    args = [o_hgrn, o_attn, x, w, gain2]
    in_specs = [row(o_hgrn.shape[1]), row(o_attn.shape[1]), row(D), whole(w), whole(gain2)]
    out_specs = [row(D), row(D)]
    out_shape = [jax.ShapeDtypeStruct((T, D), F32), jax.ShapeDtypeStruct((T, D), F32 if route else BF16)]
    if route:
        wr = jnp.zeros((D, HEAD_DIM), F32).at[:, :N_EXPERTS].set(w_router.astype(F32))
        wr_hi = wr.astype(BF16)
        wr = jnp.stack([wr_hi, (wr - wr_hi.astype(F32)).astype(BF16)])
        args.append(wr)
        in_specs.append(pl.BlockSpec(wr.shape, lambda i: (0, 0, 0), pipeline_mode=pl.Buffered(1)))
        out_specs += [row(HEAD_DIM), row(HEAD_DIM)]
        out_shape += [jax.ShapeDtypeStruct((T, HEAD_DIM), F32),
                      jax.ShapeDtypeStruct((T, HEAD_DIM), jnp.int32)]
    return pl.pallas_call(
        functools.partial(_out_proj_kernel, route=route, sub_rows=sub_rows),
        grid=(T // tm,),
        in_specs=in_specs,
        out_specs=out_specs,
        out_shape=out_shape,
        compiler_params=_params(("parallel",)),
        name="out_proj_route" if route else "out_proj",
    )(*args)


def _fresh_weights(be_ref, i):
    return (i == 0) | (be_ref[i] != be_ref[jnp.maximum(i - 1, 0)])


def _ffn_up_kernel(be_ref, na_ref, x_ref, wg_ref, wu_ref, a_ref, wg16, wu16):
    i = pl.program_id(1)
    active = i < na_ref[0]

    @pl.when(active & _fresh_weights(be_ref, i))
    def _():
        wg16[...] = wg_ref[...].astype(BF16)
        wu16[...] = wu_ref[...].astype(BF16)

    @pl.when(active)
    def _():
        x = x_ref[...]
        g = jnp.dot(x, wg16[...], preferred_element_type=F32)
        u = jnp.dot(x, wu16[...], preferred_element_type=F32)
        a_ref[...] = (_silu(g) * u).astype(a_ref.dtype)

    @pl.when(jnp.logical_not(active))
    def _():
        a_ref[...] = jnp.zeros_like(a_ref)


def _ffn_down_kernel(*refs, residual):
    if residual:
        be_ref, na_ref, a_ref, wd_ref, x_ref, o_ref, wd16 = refs
    else:
        be_ref, na_ref, a_ref, wd_ref, o_ref, wd16 = refs
    i = pl.program_id(1)
    active = i < na_ref[0]

    @pl.when(active & _fresh_weights(be_ref, i))
    def _():
        wd16[...] = wd_ref[...].astype(BF16)

    @pl.when(active)
    def _():
        y = jnp.dot(a_ref[...], wd16[...], preferred_element_type=F32)
        o_ref[...] = x_ref[...] + y if residual else y

    @pl.when(jnp.logical_not(active))
    def _():
        o_ref[...] = jnp.zeros_like(o_ref)


def _block(i, na):
    return jnp.minimum(i, na[0] - 1)


def ffn_up(rows, block_expert, n_active, w_gate, w_up, *, tm, tf):
    R, D = rows.shape
    F = w_gate.shape[2]
    tf = min(tf, F)
    wspec =pl.BlockSpec((None, D, tf), lambda f, i, be, na: (be[_block(i, na)], 0, f))
    return pl.pallas_call(
        _ffn_up_kernel,
        grid_spec=pltpu.PrefetchScalarGridSpec(
            num_scalar_prefetch=2,
            grid=(F // tf, R // tm),
            in_specs=[pl.BlockSpec((tm, D), lambda f, i, be, na: (_block(i, na), 0)), wspec, wspec],
            out_specs=pl.BlockSpec((tm, tf), lambda f, i, be, na: (i, f)),
            scratch_shapes=[pltpu.VMEM((D, tf), BF16), pltpu.VMEM((D, tf), BF16)]),
        out_shape=jax.ShapeDtypeStruct((R, F), BF16),
        compiler_params=_params(("arbitrary", "arbitrary")),
        name="ffn_up",
    )(block_expert, n_active, rows, w_gate, w_up)


def ffn_down(act, block_expert, n_active, w_down, x=None, *, tm, tn):
    R, F = act.shape
    D = w_down.shape[2]
    tn = min(tn, D)
    residual = x is not None
    tile = pl.BlockSpec((tm, tn), lambda n, i, be, na: (i, n))
    in_specs = [pl.BlockSpec((tm, F), lambda n, i, be, na: (_block(i, na), 0)),
                pl.BlockSpec((None, F, tn), lambda n, i, be, na: (be[_block(i, na)], 0, n))]
    args = [act, w_down]
    if residual:
        in_specs.append(tile)
        args.append(x)
    return pl.pallas_call(
        functools.partial(_ffn_down_kernel, residual=residual),
        grid_spec=pltpu.PrefetchScalarGridSpec(
            num_scalar_prefetch=2,
            grid=(D // tn, R // tm),
            in_specs=in_specs,
            out_specs=tile,
            scratch_shapes=[pltpu.VMEM((F, tn), BF16)]),
        out_shape=jax.ShapeDtypeStruct((R, D), F32),
        compiler_params=_params(("arbitrary", "arbitrary")),
        name="ffn_down",
    )(block_expert, n_active, *args)


def dense_ffn(h, x, w_gate, w_up, w_down, index, *, tm=1024, tf=512, tm_down=512, tn=512):
    T = x.shape[0]

    def blocks(t):
        return jnp.full((T // t,), index, jnp.int32), jnp.full((1,), T // t, jnp.int32)

    act = ffn_up(h, *blocks(tm), w_gate, w_up, tm=tm, tf=tf)
    return ffn_down(act, *blocks(tm_down), w_down, x, tm=tm_down, tn=tn)


def _row_copy(src_hbm, dst_ref, sem, src_row, group, j):
    return pltpu.make_async_copy(src_hbm.at[pl.ds(src_row, 1), :], dst_ref.at[group, pl.ds(j, 1), :], sem)


def _gather_rows_kernel(na_ref, idx_ref, nxt_ref, h_hbm, o_ref, buf, sem):
    tm = buf.shape[1] * SUBLANES
    i = pl.program_id(0)
    slot = i % 2

    def rows(ids_ref, s, start):
        def body(g, c):
            for j in range(SUBLANES):
                cp = _row_copy(h_hbm, buf.at[s], sem.at[s], ids_ref[0, 0, g * SUBLANES + j], g, j)
                if start:
                    cp.start(priority=j % 2)
                else:
                    cp.wait()
            return c

        lax.fori_loop(0, tm // SUBLANES, body, 0)

    @pl.when(i == 0)
    def _():
        rows(idx_ref, 0, True)

    @pl.when(i + 1 < na_ref[0])
    def _():
        rows(nxt_ref, 1 - slot, True)

    @pl.when(i < na_ref[0])
    def _():
        rows(idx_ref, slot, False)
        o_ref[...] = buf[slot].reshape(o_ref.shape).astype(o_ref.dtype)

    @pl.when(i >= na_ref[0])
    def _():
        o_ref[...] = jnp.zeros_like(o_ref)


def gather_rows(h, row_token, n_active, *, tm):
    T, D = h.shape
    R = row_token.shape[0]
    ids = row_token.reshape(R // tm, 1, tm)
    return pl.pallas_call(
        _gather_rows_kernel,
        grid_spec=pltpu.PrefetchScalarGridSpec(
            num_scalar_prefetch=1,
            grid=(R // tm,),
            in_specs=[pl.BlockSpec((1, 1, tm), lambda i, na: (_block(i, na), 0, 0),
                                   memory_space=pltpu.SMEM),
                      pl.BlockSpec((1, 1, tm), lambda i, na: (_block(i + 1, na), 0, 0),
                                   memory_space=pltpu.SMEM),
                      pl.BlockSpec(memory_space=pl.ANY)],
            out_specs=pl.BlockSpec((tm, D), lambda i, na: (i, 0)),
            scratch_shapes=[pltpu.VMEM((2, tm // SUBLANES, SUBLANES, D), F32),
                            pltpu.SemaphoreType.DMA((2,))]),
        out_shape=jax.ShapeDtypeStruct((R, D), BF16),
        compiler_params=_params(("arbitrary",)),
        name="gather_rows",
    )(n_active, ids, ids, h)


def _combine_kernel(*refs, final):
    if final:
        dest_ref, x_ref, gate_ref, gain_ref, y_hbm, o_ref, buf0, buf1, sem = refs
    else:
        dest_ref, x_ref, gate_ref, y_hbm, o_ref, buf0, buf1, sem = refs
    tm = buf0.shape[0] * SUBLANES

    def copies(g, j):
        r = g * SUBLANES + j
        return (_row_copy(y_hbm, buf0, sem.at[0], dest_ref[0, 0, 2 * r], g, j),
                _row_copy(y_hbm, buf1, sem.at[1], dest_ref[0, 0, 2 * r + 1], g, j))

    def start(g, c):
        for j in range(SUBLANES):
            first, second = copies(g, j)
            first.start(priority=0)
            second.start(priority=1)
        return c

    def wait(g, c):
        for j in range(SUBLANES):
            for cp in copies(g, j):
                cp.wait()
        return c

    lax.fori_loop(0, tm // SUBLANES, start, 0)
    lax.fori_loop(0, tm // SUBLANES, wait, 0)
    gates = gate_ref[...]
    x = (x_ref[...] + gates[:, 0:1] * buf0[...].reshape(x_ref.shape)
         + gates[:, 1:2] * buf1[...].reshape(x_ref.shape))
    o_ref[...] = _rms(x, gain_ref[...]) if final else x


def combine_rows(x, y_rows, dest, gates, final_gain=None, *, tm=512):
    T, D = x.shape
    tm = min(tm, T)
    final = final_gain is not None
    row = pl.BlockSpec((tm, D), lambda i: (i, 0))
    in_specs = [pl.BlockSpec((1, 1, 2 * tm), lambda i: (i, 0, 0), memory_space=pltpu.SMEM),
                row, pl.BlockSpec((tm, gates.shape[1]), lambda i: (i, 0))]
    args = [dest.reshape(T // tm, 1, 2 * tm), x, gates]
    if final:
        in_specs.append(pl.BlockSpec((1, D), lambda i: (0, 0)))
        args.append(final_gain.reshape(1, D).astype(F32))
    in_specs.append(pl.BlockSpec(memory_space=pl.ANY))
    args.append(y_rows)
    return pl.pallas_call(
        functools.partial(_combine_kernel, final=final),
        grid=(T // tm,),
        in_specs=in_specs,
        out_specs=row,
        out_shape=jax.ShapeDtypeStruct((T, D), F32),
        scratch_shapes=[pltpu.VMEM((tm // SUBLANES, SUBLANES, D), F32),
                        pltpu.VMEM((tm // SUBLANES, SUBLANES, D), F32),
                        pltpu.SemaphoreType.DMA((2,))],
        compiler_params=_params(("arbitrary",)),
        name="combine_rows",
    )(*args)


def _rms_rows_kernel(x_ref, g_ref, o_ref):
    o_ref[...] = _rms(x_ref[...], g_ref[...])


def rms_rows(x, gain, *, tm=512):
    T, D = x.shape
    tm = min(tm, T)
    return pl.pallas_call(
        _rms_rows_kernel,
        grid=(T // tm,),
        in_specs=[pl.BlockSpec((tm, D), lambda i: (i, 0)), pl.BlockSpec((1, D), lambda i: (0, 0))],
        out_specs=pl.BlockSpec((tm, D), lambda i: (i, 0)),
        out_shape=jax.ShapeDtypeStruct((T, D), F32),
        compiler_params=_params(("parallel",)),
        name="final_norm",
    )(x, gain.reshape(1, D).astype(F32))


def moe_layer(h, x, gates, top_idx, w_gate, w_up, w_down, index, final_gain=None, *,
              tm=512, tm_down=512):
    T, D = x.shape
    E = w_gate.shape[1]
    w_gate, w_up, w_down = (w.reshape((-1,) + w.shape[2:]) for w in (w_gate, w_up, w_down))
    n_assign = 2 * T
    flat_expert = top_idx[:, :2].reshape(n_assign)
    onehot = (flat_expert[:, None] == jnp.arange(E, dtype=jnp.int32)[None, :]).astype(jnp.int32)
    csum = jnp.cumsum(onehot, axis=0)
    rank = jnp.take_along_axis(csum, flat_expert[:, None], axis=1)[:, 0] - 1
    counts = csum[-1]
    padded = (counts + tm - 1) // tm * tm
    padded_ends = jnp.cumsum(padded)
    dest = (padded_ends - padded)[flat_expert] + rank
    n_blocks = -(-n_assign // tm) + E
    row_token = jnp.zeros((n_blocks * tm,), jnp.int32).at[dest].set(
        jnp.arange(n_assign, dtype=jnp.int32) // 2)
    block_expert = jnp.minimum(
        jnp.searchsorted(padded_ends, jnp.arange(n_blocks, dtype=jnp.int32) * tm, side='right'),
        E - 1).astype(jnp.int32) + index * E
    n_active = (padded_ends[-1:] // tm).astype(jnp.int32)
    rows = gather_rows(h, row_token, n_active, tm=tm)
    act = ffn_up(rows, block_expert, n_active, w_gate, w_up, tm=tm, tf=1024)
    sub = tm // tm_down
    y_rows = ffn_down(act, jnp.repeat(block_expert, sub), n_active * sub, w_down, tm=tm_down, tn=512)
    return combine_rows(x, y_rows, dest.astype(jnp.int32), gates, final_gain)


def kernel(x, norm_mix, w_in, lb_logits, hgrn_norm, w_out, norm_ffn, dense_w_gate, dense_w_up,
           dense_w_down, moe_router, moe_w_gate, moe_w_up, moe_w_down, final_norm):
    B, S, D = x.shape
    depth = w_in.shape[0]
    lb_sm = jax.nn.softmax(lb_logits.astype(F32), axis=0)
    lower_bounds = jnp.cumsum(lb_sm, axis=0) - lb_sm[0:1]
    xt = x.reshape(B * S, D)
    for layer in range(depth):
        proj = norm_matmul(xt, norm_mix[layer], w_in, layer)
        o_hgrn = hgrn2(proj, lower_bounds[layer], hgrn_norm[layer], batch=B, seq=S)
        o_attn = dilated_attention(proj, batch=B, seq=S, col0=4 * HGRN_WIDTH)
        i = layer // 2
        last = layer == depth - 1
        if layer % 2 == 0:
            xt, h = out_proj_norm(o_hgrn, o_attn, xt, w_out[layer].astype(BF16), norm_ffn[layer])
            xt = dense_ffn(h, xt, dense_w_gate, dense_w_up, dense_w_down, i)
            if last:
                xt = rms_rows(xt, final_norm)
        else:
            xt, h, gates, idx = out_proj_norm(o_hgrn, o_attn, xt, w_out[layer].astype(BF16),
                                              norm_ffn[layer], moe_router[i])
            xt = moe_layer(h, xt, gates, idx, moe_w_gate, moe_w_up, moe_w_down, i,
                           final_norm if last else None)
    return xt.reshape(B, S, D)
```

```python
import functools
import math

import jax
import jax.numpy as jnp
from jax import lax
from jax.experimental import pallas as pl
from jax.experimental.pallas import tpu as pltpu

F32 = jnp.float32
BF16 = jnp.bfloat16

HEAD_DIM = 128
N_HGRN_HEADS = 8
N_ATTN_HEADS = 8
HGRN_WIDTH = N_HGRN_HEADS * HEAD_DIM
ATTN_WIDTH = N_ATTN_HEADS * HEAD_DIM
HGRN_CHUNK = 64
HGRN_SUB = 8
HGRN_CHUNKS_PER_ITER = 8
HGRN_HEADS_PER_STEP = 8
DILATED_PATTERNS = ((128, 1), (512, 4), (2048, 16))
ATTN_CLASSES = 4
ATTN_PAD_N = 128
ATTN_PAD_M = 512
ATTN_UNITS = 32
SUBLANES = 8
OUT_PROJ_SUB_ROWS = 256
ROPE_THETA = 10000.0
N_EXPERTS = 8
NORM_EPS = 1e-6
VMEM_LIMIT = 60 * 1024 * 1024

_NT = (((1,), (1,)), ((), ()))
_TN = (((0,), (0,)), ((), ()))


def _params(semantics):
    return pltpu.CompilerParams(dimension_semantics=semantics, vmem_limit_bytes=VMEM_LIMIT)


def _silu(z):
    return z * jax.nn.sigmoid(z)


def _rms(z, gain):
    ms = jnp.mean(z * z, axis=-1, keepdims=True)
    return z * lax.rsqrt(ms + NORM_EPS) * gain


def _in_proj_kernel(x_ref, g_ref, w_ref, o_ref, h_ref):
    @pl.when(pl.program_id(1) == 0)
    def _():
        h_ref[...] = _rms(x_ref[...], g_ref[...]).astype(BF16)

    o_ref[...] = jnp.dot(h_ref[...], w_ref[...].astype(BF16),
                         preferred_element_type=F32).astype(o_ref.dtype)


def norm_matmul(x, gain, w, index, *, tm=1024, tn=1024):
    T, D = x.shape
    N = w.shape[2]
    tm, tn = min(tm, T), min(tn, N)
    return pl.pallas_call(
        _in_proj_kernel,
        grid=(T // tm, N // tn),
        in_specs=[pl.BlockSpec((tm, D), lambda i, j: (i, 0)),
                  pl.BlockSpec((1, D), lambda i, j: (0, 0)),
                  pl.BlockSpec((None, D, tn), lambda i, j: (index, 0, j))],
        out_specs=pl.BlockSpec((tm, tn), lambda i, j: (i, j)),
        out_shape=jax.ShapeDtypeStruct((T, N), BF16),
        scratch_shapes=[pltpu.VMEM((tm, D), BF16)],
        compiler_params=_params(("parallel", "arbitrary")),
        name="in_proj",
    )(x, gain.reshape(1, D).astype(F32), w)


def _hgrn_kernel(q_ref, f_ref, i_ref, g_ref, lb_ref, gain_ref, o_ref,
                 st_ref, k_ref, tril_ref, select_ref, causal_ref, *, n_chunks, heads, chunks_per_iter):
    C, SUB = HGRN_CHUNK, HGRN_SUB

    @pl.when(pl.program_id(2) == 0)
    def _():
        st_ref[...] = jnp.zeros_like(st_ref)
        row = lax.broadcasted_iota(jnp.int32, (C, C), 0)
        col = lax.broadcasted_iota(jnp.int32, (C, C), 1)
        tril_ref[...] = (col <= row).astype(BF16)
        group = lax.shift_right_logical(
            lax.broadcasted_iota(jnp.int32, (SUB * HEAD_DIM, HEAD_DIM), 0), HEAD_DIM.bit_length() - 1)
        select_ref[...] = (group == lax.broadcasted_iota(jnp.int32, (SUB * HEAD_DIM, HEAD_DIM), 1)
                           ).astype(BF16)
        sub_row = lax.broadcasted_iota(jnp.int32, (SUB, SUB, HEAD_DIM), 1)
        key = lax.broadcasted_iota(jnp.int32, (SUB, SUB, HEAD_DIM), 0)
        causal_ref[...] = jnp.where(sub_row >= key, 0.0, -jnp.inf)

    pairs = []
    span = C
    while span > SUB:
        half = span // 2
        pairs += [(base + half, base + span, base, base + half) for base in range(0, C, span)]
        span = half

    def gates(hh, rows):
        lanes = slice(hh * HEAD_DIM, (hh + 1) * HEAD_DIM)
        lb = lb_ref[:, lanes]
        f = lb + (1.0 - lb) * jax.nn.sigmoid(f_ref[rows, lanes].astype(F32))
        g = jnp.log2(f)
        g_hi = g.astype(BF16)
        g_lo = (g - g_hi.astype(F32)).astype(BF16)
        tril = tril_ref[...]
        b = (jnp.dot(tril, g_hi, preferred_element_type=F32)
             + jnp.dot(tril, g_lo, preferred_element_type=F32))
        return dict(lanes=lanes, b=b, kk=1.0 - f, q=_silu(q_ref[rows, lanes].astype(F32)),
                    v=i_ref[rows, lanes].astype(F32))

    def scores(hh, slot, s):
        b, kk, q, v = s["b"], s["kk"], s["q"], s["v"]
        k_scr = k_ref.at[slot * heads + hh]
        k_scr[...] = jnp.log2(kk) - b
        s["v16"] = v.astype(BF16)
        b_end = b[C - 1:C, :]
        st = st_ref[hh]
        s["o"] = lax.dot_general((q * jnp.exp2(b)).astype(BF16), st.astype(BF16), _NT,
                                 preferred_element_type=F32)
        kd = (kk * jnp.exp2(b_end - b)).astype(BF16)
        st_ref[hh] = st * jnp.exp2(b_end) + lax.dot_general(s["v16"], kd, _TN,
                                                            preferred_element_type=F32)
        s["a_off"] = []
        for t0, t1, s0, s1 in pairs:
            b_mid = b[s1 - 1:s1, :]
            qa = (q[t0:t1] * jnp.exp2(b[t0:t1] - b_mid)).astype(BF16)
            ka = (kk[s0:s1] * jnp.exp2(b_mid - b[s0:s1])).astype(BF16)
            s["a_off"].append(lax.dot_general(qa, ka, _NT, preferred_element_type=F32))
        blocks = []
        for r in range(0, C, SUB):
            qs = q[r:r + SUB]
            bs = b[r:r + SUB]
            groups = []
            for j in range(SUB):
                d = bs + jnp.broadcast_to(k_scr[r + j:r + j + 1, :], (SUB, HEAD_DIM))
                if j:
                    d = d + causal_ref[j]
                groups.append(qs * jnp.exp2(d))
            blocks.append(jnp.concatenate(groups, axis=1))
        a = jnp.dot(jnp.concatenate(blocks, axis=0).astype(BF16), select_ref[...],
                    preferred_element_type=F32)
        s["a_diag"] = jnp.concatenate(
            [a[r:r + SUB] if r == 0 else pltpu.roll(a[r:r + SUB], r, 1) for r in range(0, C, SUB)], axis=0)

    def values(hh, s):
        parts = [jnp.zeros((SUB, HEAD_DIM), F32)] + [None] * (C // SUB - 1)

        def add(idx, piece):
            parts[idx] = piece if parts[idx] is None else parts[idx] + piece

        for (t0, t1, s0, s1), a in zip(pairs, s["a_off"]):
            blk = jnp.dot(a.astype(BF16), s["v16"][s0:s1], preferred_element_type=F32)
            for j in range((t1 - t0) // SUB):
                add(t0 // SUB + j, blk[j * SUB:(j + 1) * SUB])
        v_rows = jnp.concatenate([s["v16"], jnp.zeros((HEAD_DIM - C, HEAD_DIM), BF16)], axis=0)
        diag = jnp.dot(s["a_diag"].astype(BF16), v_rows, preferred_element_type=F32)
        s["o"] = s["o"] + diag + jnp.concatenate(parts, axis=0)

    def finish(s, rows):
        lanes = s["lanes"]
        o = _rms(s["o"], gain_ref[:, lanes]) * _silu(g_ref[rows, lanes].astype(F32))
        o_ref[rows, lanes] = o.astype(o_ref.dtype)

    def chunk(ci, carry):
        work = []
        for slot in range(chunks_per_iter):
            rows = pl.ds(pl.multiple_of((ci * chunks_per_iter + slot) * C, C), C)
            work += [(hh, slot, rows, gates(hh, rows)) for hh in range(heads)]
        for hh, slot, _, s in work:
            scores(hh, slot, s)
        for hh, _, _, s in work:
            values(hh, s)
        for _, _, rows, s in work:
            finish(s, rows)
        return carry

    lax.fori_loop(0, n_chunks // chunks_per_iter, chunk, 0)


def hgrn2(proj, lower_bound, out_gain, *, batch, seq, tt=512, heads=HGRN_HEADS_PER_STEP,
          chunks_per_iter=HGRN_CHUNKS_PER_ITER):
    T = proj.shape[0]
    H = N_HGRN_HEADS // heads
    width = heads * HEAD_DIM
    tt = min(tt, seq)
    nt = seq // tt
    assert tt % (HGRN_CHUNK * chunks_per_iter) == 0

    def sec(k):
        return pl.BlockSpec((tt, width), lambda b, h, c, k=k: (b * nt + c, k * H + h))

    vec = pl.BlockSpec((1, width), lambda b, h, c: (0, h))
    per_head = pltpu.VMEM((chunks_per_iter * heads, HGRN_CHUNK, HEAD_DIM), F32)
    return pl.pallas_call(
        functools.partial(_hgrn_kernel, n_chunks=tt // HGRN_CHUNK, heads=heads,
                          chunks_per_iter=chunks_per_iter),
        grid=(batch, H, nt),
        in_specs=[sec(0), sec(1), sec(2), sec(3), vec, vec],
        out_specs=pl.BlockSpec((tt, width), lambda b, h, c: (b * nt + c, h)),
        out_shape=jax.ShapeDtypeStruct((T, HGRN_WIDTH), BF16),
        scratch_shapes=[pltpu.VMEM((heads, HEAD_DIM, HEAD_DIM), F32), per_head,
                        pltpu.VMEM((HGRN_CHUNK, HGRN_CHUNK), BF16),
                        pltpu.VMEM((HGRN_SUB * HEAD_DIM, HEAD_DIM), BF16),
                        pltpu.VMEM((HGRN_SUB, HGRN_SUB, HEAD_DIM), F32)],
        compiler_params=_params(("parallel", "parallel", "arbitrary")),
        name="hgrn2",
    )(proj, proj, proj, proj,
      lower_bound.reshape(1, HGRN_WIDTH).astype(F32), out_gain.reshape(1, HGRN_WIDTH).astype(F32))


def _attn_kernel(q_ref, k_ref, v_ref, cos_ref, sin_ref, o_ref,
                 qn, kn, vn, qm, km, vm, o1, o2, o3, l1, l2, l3, bias_ref, *, seq, n_par):
    BLK = 128
    ROWS = 512
    NC = ATTN_CLASSES
    per_class = seq // NC
    scale = HEAD_DIM ** -0.5 * math.log2(math.e)

    def rope(i, carry):
        rows = pl.ds(pl.multiple_of(i * ROWS, ROWS), ROWS)
        c = cos_ref[rows, :]
        s = sin_ref[rows, :]
        q = q_ref[rows, :].astype(F32)
        k = k_ref[rows, :].astype(F32)
        qn[rows, :] = (q * c + pltpu.roll(q, HEAD_DIM // 2, 1) * s) * scale
        padded = pl.ds(pl.multiple_of(i * ROWS, ROWS) + ATTN_PAD_N, ROWS)
        kn[padded, :] = k * c + pltpu.roll(k, HEAD_DIM // 2, 1) * s
        vn[padded, :] = v_ref[rows, :].astype(F32)
        return carry

    lax.fori_loop(0, seq // ROWS, rope, 0)

    kn[0:ATTN_PAD_N, :] = jnp.zeros((ATTN_PAD_N, HEAD_DIM), F32)
    vn[0:ATTN_PAD_N, :] = jnp.zeros((ATTN_PAD_N, HEAD_DIM), F32)
    for c in range(NC):
        base = c * (ATTN_PAD_M + per_class)
        km[base:base + ATTN_PAD_M, :] = jnp.zeros((ATTN_PAD_M, HEAD_DIM), F32)
        vm[base:base + ATTN_PAD_M, :] = jnp.zeros((ATTN_PAD_M, HEAD_DIM), F32)

    def regroup(i, carry):
        j0 = pl.multiple_of(i * ROWS, ROWS)
        for c in range(NC):
            src = pl.ds(NC * j0 + c, ROWS, stride=NC)
            qm[pl.ds(c * per_class + j0, ROWS), :] = qn[src, :]
            dst = pl.ds(c * (ATTN_PAD_M + per_class) + ATTN_PAD_M + j0, ROWS)
            km[dst, :] = kn[pl.ds(NC * j0 + c + ATTN_PAD_N, ROWS, stride=NC), :]
            vm[dst, :] = vn[pl.ds(NC * j0 + c + ATTN_PAD_N, ROWS, stride=NC), :]
        return carry

    lax.fori_loop(0, per_class // ROWS, regroup, 0)

    qi = lax.broadcasted_iota(jnp.int32, (BLK, 2 * BLK), 0)
    kj = lax.broadcasted_iota(jnp.int32, (BLK, 2 * BLK), 1)
    band = (kj >= qi) & (kj <= qi + BLK)
    bias_ref[0] = jnp.where(band & (kj >= BLK), 0.0, -jnp.inf)
    bias_ref[1] = jnp.where(band, 0.0, -jnp.inf)
    ones = jnp.ones((2 * BLK, HEAD_DIM), BF16)

    def ds(start, size, stride):
        return pl.ds(start, size) if stride == 1 else pl.ds(start, size, stride=stride)

    def unit_rows(d, u):
        if d == 1:
            return u, pl.ds(u * BLK, BLK), pl.ds(u * BLK + (ATTN_PAD_N - BLK), 2 * BLK), pl.ds(u * BLK, BLK)
        step = d // NC
        nb = per_class // (BLK * step)
        n = u & (nb - 1)
        r = lax.shift_right_logical(u, nb.bit_length() - 1) & (step - 1)
        c = lax.shift_right_logical(u, (nb * step).bit_length() - 1)
        j0 = n * (BLK * step) + r
        return (n, ds(c * per_class + j0, BLK, step),
                ds(c * (ATTN_PAD_M + per_class) + (ATTN_PAD_M - BLK * step) + j0, 2 * BLK, step),
                pl.ds(NC * j0 + c, BLK, stride=d))

    def qk(d, u):
        q_src, k_src, v_src = (qn, kn, vn) if d == 1 else (qm, km, vm)
        n, q_rows, k_rows, out_rows = unit_rows(d, u)
        q = q_src[q_rows, :].astype(BF16)
        k = k_src[k_rows, :].astype(BF16)
        v = v_src[k_rows, :].astype(BF16)
        s = lax.dot_general(q, k, _NT, preferred_element_type=F32)
        return dict(n=n, out_rows=out_rows, s=s, v=v)

    def pv(t):
        s = t["s"] + bias_ref[jnp.minimum(t["n"], 1)]
        t["m"] = jnp.max(s, axis=-1, keepdims=True)
        p = jnp.exp2(s - t["m"]).astype(BF16)
        t["acc"] = jnp.dot(p, jnp.concatenate([t["v"], ones], axis=1), preferred_element_type=F32)

    def store(t, o_out, l_out):
        den = t["acc"][:, HEAD_DIM:]
        o_out[t["out_rows"], :] = t["acc"][:, :HEAD_DIM] / den
        l_out[t["out_rows"], :] = t["m"] + jnp.log2(den)

    for (window, d), o_out, l_out in zip(DILATED_PATTERNS, (o1, o2, o3), (l1, l2, l3)):
        assert window == BLK * d and (d == 1 or d % NC == 0)
        assert BLK * d <= (ATTN_PAD_N if d == 1 else ATTN_PAD_M * NC) and per_class % (BLK * max(d // NC, 1)) == 0
        n_units = seq // BLK
        assert n_units % n_par == 0

        def units(it, carry, d=d, o_out=o_out, l_out=l_out):
            ts = [qk(d, it * n_par + j) for j in range(n_par)]
            for t in ts:
                pv(t)
            for t in ts:
                store(t, o_out, l_out)
            return carry

        lax.fori_loop(0, n_units // n_par, units, 0)

    def combine(i, carry):
        rows = pl.ds(pl.multiple_of(i * ROWS, ROWS), ROWS)
        a1, a2, a3 = l1[rows, :], l2[rows, :], l3[rows, :]
        m = jnp.maximum(jnp.maximum(a1, a2), a3)
        e1, e2, e3 = jnp.exp2(a1 - m), jnp.exp2(a2 - m), jnp.exp2(a3 - m)
        out = (e1 * o1[rows, :] + e2 * o2[rows, :] + e3 * o3[rows, :]) / (e1 + e2 + e3)
        o_ref[rows, :] = out.astype(o_ref.dtype)
        return carry

    lax.fori_loop(0, seq // ROWS, combine, 0)


def _rope_tables(seq):
    half = HEAD_DIM // 2
    inv_freq = ROPE_THETA ** (-jnp.arange(half, dtype=F32) / half)
    ang = jnp.arange(seq, dtype=F32)[:, None] * inv_freq[None, :]
    cos, sin = jnp.cos(ang), jnp.sin(ang)
    return jnp.concatenate([cos, cos], axis=-1), jnp.concatenate([-sin, sin], axis=-1)


def dilated_attention(proj, *, batch, seq, col0, n_par=ATTN_UNITS):
    T = proj.shape[0]
    H = N_ATTN_HEADS
    c0 = col0 // HEAD_DIM
    cos, sin = _rope_tables(seq)

    def sec(k):
        return pl.BlockSpec((seq, HEAD_DIM), lambda b, h, k=k: (b, c0 + k * H + h))

    table = pl.BlockSpec((seq, HEAD_DIM), lambda b, h: (0, 0), pipeline_mode=pl.Buffered(1))
    slab = pltpu.VMEM((seq, HEAD_DIM), F32)
    padded_n = pltpu.VMEM((ATTN_PAD_N + seq, HEAD_DIM), F32)
    padded_m = pltpu.VMEM((ATTN_CLASSES * ATTN_PAD_M + seq, HEAD_DIM), F32)
    return pl.pallas_call(
        functools.partial(_attn_kernel, seq=seq, n_par=n_par),
        grid=(batch, H),
        in_specs=[sec(0), sec(1), sec(2), table, table],
        out_specs=pl.BlockSpec((seq, HEAD_DIM), lambda b, h: (b, h)),
        out_shape=jax.ShapeDtypeStruct((T, ATTN_WIDTH), BF16),
        scratch_shapes=[slab, padded_n, padded_n, slab, padded_m, padded_m] + [slab] * 6
                       + [pltpu.VMEM((2, HEAD_DIM, 2 * HEAD_DIM), F32)],
        compiler_params=_params(("parallel", "parallel")),
        name="dilated_attn",
    )(proj, proj, proj, cos, sin)


def _out_proj_kernel(*refs, route, sub_rows):
    if route:
        oh_ref, oa_ref, x_ref, w_ref, g_ref, wr_ref, xo_ref, h_ref, gate_ref, idx_ref = refs
    else:
        oh_ref, oa_ref, x_ref, w_ref, g_ref, xo_ref, h_ref = refs
    hw = oh_ref.shape[1]
    tm = x_ref.shape[0]
    sub = min(tm, sub_rows)
    for r0 in range(0, tm, sub):
        rows = slice(r0, r0 + sub)
        y = (jnp.dot(oh_ref[rows, :], w_ref[:hw, :], preferred_element_type=F32)
             + jnp.dot(oa_ref[rows, :], w_ref[hw:, :], preferred_element_type=F32))
        x = x_ref[rows, :] + y
        xo_ref[rows, :] = x
        h = _rms(x, g_ref[...])
        h_ref[rows, :] = h.astype(h_ref.dtype)
        if route:
            h_hi = h.astype(BF16)
            h_lo = (h - h_hi.astype(F32)).astype(BF16)
            logits = (jnp.dot(h_hi, wr_ref[0], preferred_element_type=F32)
                      + jnp.dot(h_lo, wr_ref[0], preferred_element_type=F32)
                      + jnp.dot(h_hi, wr_ref[1], preferred_element_type=F32))
            lane = lax.broadcasted_iota(jnp.int32, logits.shape, 1)
            lg = jnp.where(lane < N_EXPERTS, logits, -jnp.inf)
            m1 = jnp.max(lg, axis=-1, keepdims=True)
            i1 = jnp.min(jnp.where(lg == m1, lane, HEAD_DIM), axis=-1, keepdims=True)
            lg = jnp.where(lane == i1, -jnp.inf, lg)
            m2 = jnp.max(lg, axis=-1, keepdims=True)
            i2 = jnp.min(jnp.where(lg == m2, lane, HEAD_DIM), axis=-1, keepdims=True)
            e = jnp.exp(m2 - m1)
            gate_ref[rows, :] = jnp.where(lane == 0, 1.0 / (1.0 + e),
                                          jnp.where(lane == 1, e / (1.0 + e), 0.0))
            idx_ref[rows, :] = jnp.where(lane == 0, i1, jnp.where(lane == 1, i2, 0))


def out_proj_norm(o_hgrn, o_attn, x, w, gain, w_router=None, *, tm=512, sub_rows=OUT_PROJ_SUB_ROWS):
    T, D = x.shape
    tm = min(tm, T)
    route = w_router is not None
    row = lambda width: pl.BlockSpec((tm, width), lambda i: (i, 0))
    whole = lambda a: pl.BlockSpec(a.shape, lambda i: (0, 0), pipeline_mode=pl.Buffered(1))
    gain2 = gain.reshape(1, D).astype(F32)
    args = [o_hgrn, o_attn, x, w, gain2]
    in_specs = [row(o_hgrn.shape[1]), row(o_attn.shape[1]), row(D), whole(w), whole(gain2)]
    out_specs = [row(D), row(D)]
    out_shape = [jax.ShapeDtypeStruct((T, D), F32), jax.ShapeDtypeStruct((T, D), F32 if route else BF16)]
    if route:
        wr = jnp.zeros((D, HEAD_DIM), F32).at[:, :N_EXPERTS].set(w_router.astype(F32))
        wr_hi = wr.astype(BF16)
        wr = jnp.stack([wr_hi, (wr - wr_hi.astype(F32)).astype(BF16)])
        args.append(wr)
        in_specs.append(pl.BlockSpec(wr.shape, lambda i: (0, 0, 0), pipeline_mode=pl.Buffered(1)))
        out_specs += [row(HEAD_DIM), row(HEAD_DIM)]
        out_shape += [jax.ShapeDtypeStruct((T, HEAD_DIM), F32),
                      jax.ShapeDtypeStruct((T, HEAD_DIM), jnp.int32)]
    return pl.pallas_call(
        functools.partial(_out_proj_kernel, route=route, sub_rows=sub_rows),
        grid=(T // tm,),
        in_specs=in_specs,
        out_specs=out_specs,
        out_shape=out_shape,
        compiler_params=_params(("parallel",)),
        name="out_proj_route" if route else "out_proj",
    )(*args)


def _fresh_weights(be_ref, i):
    return (i == 0) | (be_ref[i] != be_ref[jnp.maximum(i - 1, 0)])


def _ffn_up_kernel(be_ref, na_ref, x_ref, wg_ref, wu_ref, a_ref, wg16, wu16):
    i = pl.program_id(1)
    active = i < na_ref[0]

    @pl.when(active & _fresh_weights(be_ref, i))
    def _():
        wg16[...] = wg_ref[...].astype(BF16)
        wu16[...] = wu_ref[...].astype(BF16)

    @pl.when(active)
    def _():
        x = x_ref[...]
        g = jnp.dot(x, wg16[...], preferred_element_type=F32)
        u = jnp.dot(x, wu16[...], preferred_element_type=F32)
        a_ref[...] = (_silu(g) * u).astype(a_ref.dtype)

    @pl.when(jnp.logical_not(active))
    def _():
        a_ref[...] = jnp.zeros_like(a_ref)


def _ffn_down_kernel(*refs, residual):
    if residual:
        be_ref, na_ref, a_ref, wd_ref, x_ref, o_ref, wd16 = refs
    else:
        be_ref, na_ref, a_ref, wd_ref, o_ref, wd16 = refs
    i = pl.program_id(1)
    active = i < na_ref[0]

    @pl.when(active & _fresh_weights(be_ref, i))
    def _():
        wd16[...] = wd_ref[...].astype(BF16)

    @pl.when(active)
    def _():
        y = jnp.dot(a_ref[...], wd16[...], preferred_element_type=F32)
        o_ref[...] = x_ref[...] + y if residual else y

    @pl.when(jnp.logical_not(active))
    def _():
        o_ref[...] = jnp.zeros_like(o_ref)


def _block(i, na):
    return jnp.minimum(i, na[0] - 1)


def ffn_up(rows, block_expert, n_active, w_gate, w_up, *, tm, tf):
    R, D = rows.shape
    F = w_gate.shape[2]
    tf = min(tf, F)
    wspec =pl.BlockSpec((None, D, tf), lambda f, i, be, na: (be[_block(i, na)], 0, f))
    return pl.pallas_call(
        _ffn_up_kernel,
        grid_spec=pltpu.PrefetchScalarGridSpec(
            num_scalar_prefetch=2,
            grid=(F // tf, R // tm),
            in_specs=[pl.BlockSpec((tm, D), lambda f, i, be, na: (_block(i, na), 0)), wspec, wspec],
            out_specs=pl.BlockSpec((tm, tf), lambda f, i, be, na: (i, f)),
            scratch_shapes=[pltpu.VMEM((D, tf), BF16), pltpu.VMEM((D, tf), BF16)]),
        out_shape=jax.ShapeDtypeStruct((R, F), BF16),
        compiler_params=_params(("arbitrary", "arbitrary")),
        name="ffn_up",
    )(block_expert, n_active, rows, w_gate, w_up)


def ffn_down(act, block_expert, n_active, w_down, x=None, *, tm, tn):
    R, F = act.shape
    D = w_down.shape[2]
    tn = min(tn, D)
    residual = x is not None
    tile = pl.BlockSpec((tm, tn), lambda n, i, be, na: (i, n))
    in_specs = [pl.BlockSpec((tm, F), lambda n, i, be, na: (_block(i, na), 0)),
                pl.BlockSpec((None, F, tn), lambda n, i, be, na: (be[_block(i, na)], 0, n))]
    args = [act, w_down]
    if residual:
        in_specs.append(tile)
        args.append(x)
    return pl.pallas_call(
        functools.partial(_ffn_down_kernel, residual=residual),
        grid_spec=pltpu.PrefetchScalarGridSpec(
            num_scalar_prefetch=2,
            grid=(D // tn, R // tm),
            in_specs=in_specs,
            out_specs=tile,
            scratch_shapes=[pltpu.VMEM((F, tn), BF16)]),
        out_shape=jax.ShapeDtypeStruct((R, D), F32),
        compiler_params=_params(("arbitrary", "arbitrary")),
        name="ffn_down",
    )(block_expert, n_active, *args)


def dense_ffn(h, x, w_gate, w_up, w_down, index, *, tm=1024, tf=512, tm_down=512, tn=512):
    T = x.shape[0]

    def blocks(t):
        return jnp.full((T // t,), index, jnp.int32), jnp.full((1,), T // t, jnp.int32)

    act = ffn_up(h, *blocks(tm), w_gate, w_up, tm=tm, tf=tf)
    return ffn_down(act, *blocks(tm_down), w_down, x, tm=tm_down, tn=tn)


def _row_copy(src_hbm, dst_ref, sem, src_row, group, j):
    return pltpu.make_async_copy(src_hbm.at[pl.ds(src_row, 1), :], dst_ref.at[group, pl.ds(j, 1), :], sem)


def _gather_rows_kernel(na_ref, idx_ref, nxt_ref, h_hbm, o_ref, buf, sem):
    tm = buf.shape[1] * SUBLANES
    i = pl.program_id(0)
    slot = i % 2

    def rows(ids_ref, s, start):
        def body(g, c):
            for j in range(SUBLANES):
                cp = _row_copy(h_hbm, buf.at[s], sem.at[s], ids_ref[0, 0, g * SUBLANES + j], g, j)
                if start:
                    cp.start(priority=j % 2)
                else:
                    cp.wait()
            return c

        lax.fori_loop(0, tm // SUBLANES, body, 0)

    @pl.when(i == 0)
    def _():
        rows(idx_ref, 0, True)

    @pl.when(i + 1 < na_ref[0])
    def _():
        rows(nxt_ref, 1 - slot, True)

    @pl.when(i < na_ref[0])
    def _():
        rows(idx_ref, slot, False)
        o_ref[...] = buf[slot].reshape(o_ref.shape).astype(o_ref.dtype)

    @pl.when(i >= na_ref[0])
    def _():
        o_ref[...] = jnp.zeros_like(o_ref)


def gather_rows(h, row_token, n_active, *, tm):
    T, D = h.shape
    R = row_token.shape[0]
    ids = row_token.reshape(R // tm, 1, tm)
    return pl.pallas_call(
        _gather_rows_kernel,
        grid_spec=pltpu.PrefetchScalarGridSpec(
            num_scalar_prefetch=1,
            grid=(R // tm,),
            in_specs=[pl.BlockSpec((1, 1, tm), lambda i, na: (_block(i, na), 0, 0),
                                   memory_space=pltpu.SMEM),
                      pl.BlockSpec((1, 1, tm), lambda i, na: (_block(i + 1, na), 0, 0),
                                   memory_space=pltpu.SMEM),
                      pl.BlockSpec(memory_space=pl.ANY)],
            out_specs=pl.BlockSpec((tm, D), lambda i, na: (i, 0)),
            scratch_shapes=[pltpu.VMEM((2, tm // SUBLANES, SUBLANES, D), F32),
                            pltpu.SemaphoreType.DMA((2,))]),
        out_shape=jax.ShapeDtypeStruct((R, D), BF16),
        compiler_params=_params(("arbitrary",)),
        name="gather_rows",
    )(n_active, ids, ids, h)


def _combine_kernel(*refs, final):
    if final:
        dest_ref, x_ref, gate_ref, gain_ref, y_hbm, o_ref, buf0, buf1, sem = refs
    else:
        dest_ref, x_ref, gate_ref, y_hbm, o_ref, buf0, buf1, sem = refs
    tm = buf0.shape[0] * SUBLANES

    def copies(g, j):
        r = g * SUBLANES + j
        return (_row_copy(y_hbm, buf0, sem.at[0], dest_ref[0, 0, 2 * r], g, j),
                _row_copy(y_hbm, buf1, sem.at[1], dest_ref[0, 0, 2 * r + 1], g, j))

    def start(g, c):
        for j in range(SUBLANES):
            first, second = copies(g, j)
            first.start(priority=0)
            second.start(priority=1)
        return c

    def wait(g, c):
        for j in range(SUBLANES):
            for cp in copies(g, j):
                cp.wait()
        return c

    lax.fori_loop(0, tm // SUBLANES, start, 0)
    lax.fori_loop(0, tm // SUBLANES, wait, 0)
    gates = gate_ref[...]
    x = (x_ref[...] + gates[:, 0:1] * buf0[...].reshape(x_ref.shape)
         + gates[:, 1:2] * buf1[...].reshape(x_ref.shape))
    o_ref[...] = _rms(x, gain_ref[...]) if final else x


def combine_rows(x, y_rows, dest, gates, final_gain=None, *, tm=512):
    T, D = x.shape
    tm = min(tm, T)
    final = final_gain is not None
    row = pl.BlockSpec((tm, D), lambda i: (i, 0))
    in_specs = [pl.BlockSpec((1, 1, 2 * tm), lambda i: (i, 0, 0), memory_space=pltpu.SMEM),
                row, pl.BlockSpec((tm, gates.shape[1]), lambda i: (i, 0))]
    args = [dest.reshape(T // tm, 1, 2 * tm), x, gates]
    if final:
        in_specs.append(pl.BlockSpec((1, D), lambda i: (0, 0)))
        args.append(final_gain.reshape(1, D).astype(F32))
    in_specs.append(pl.BlockSpec(memory_space=pl.ANY))
    args.append(y_rows)
    return pl.pallas_call(
        functools.partial(_combine_kernel, final=final),
        grid=(T // tm,),
        in_specs=in_specs,
        out_specs=row,
        out_shape=jax.ShapeDtypeStruct((T, D), F32),
        scratch_shapes=[pltpu.VMEM((tm // SUBLANES, SUBLANES, D), F32),
                        pltpu.VMEM((tm // SUBLANES, SUBLANES, D), F32),
                        pltpu.SemaphoreType.DMA((2,))],
        compiler_params=_params(("arbitrary",)),
        name="combine_rows",
    )(*args)


def _rms_rows_kernel(x_ref, g_ref, o_ref):
    o_ref[...] = _rms(x_ref[...], g_ref[...])


def rms_rows(x, gain, *, tm=512):
    T, D = x.shape
    tm = min(tm, T)
    return pl.pallas_call(
        _rms_rows_kernel,
        grid=(T // tm,),
        in_specs=[pl.BlockSpec((tm, D), lambda i: (i, 0)), pl.BlockSpec((1, D), lambda i: (0, 0))],
        out_specs=pl.BlockSpec((tm, D), lambda i: (i, 0)),
        out_shape=jax.ShapeDtypeStruct((T, D), F32),
        compiler_params=_params(("parallel",)),
        name="final_norm",
    )(x, gain.reshape(1, D).astype(F32))


def moe_layer(h, x, gates, top_idx, w_gate, w_up, w_down, index, final_gain=None, *,
              tm=512, tm_down=512):
    T, D = x.shape
    E = w_gate.shape[1]
    w_gate, w_up, w_down = (w.reshape((-1,) + w.shape[2:]) for w in (w_gate, w_up, w_down))
    n_assign = 2 * T
    flat_expert = top_idx[:, :2].reshape(n_assign)
    onehot = (flat_expert[:, None] == jnp.arange(E, dtype=jnp.int32)[None, :]).astype(jnp.int32)
    csum = jnp.cumsum(onehot, axis=0)
    rank = jnp.take_along_axis(csum, flat_expert[:, None], axis=1)[:, 0] - 1
    counts = csum[-1]
    padded = (counts + tm - 1) // tm * tm
    padded_ends = jnp.cumsum(padded)
    dest = (padded_ends - padded)[flat_expert] + rank
    n_blocks = -(-n_assign // tm) + E
    row_token = jnp.zeros((n_blocks * tm,), jnp.int32).at[dest].set(
        jnp.arange(n_assign, dtype=jnp.int32) // 2)
    block_expert = jnp.minimum(
        jnp.searchsorted(padded_ends, jnp.arange(n_blocks, dtype=jnp.int32) * tm, side='right'),
        E - 1).astype(jnp.int32) + index * E
    n_active = (padded_ends[-1:] // tm).astype(jnp.int32)
    rows = gather_rows(h, row_token, n_active, tm=tm)
    act = ffn_up(rows, block_expert, n_active, w_gate, w_up, tm=tm, tf=1024)
    sub = tm // tm_down
    y_rows = ffn_down(act, jnp.repeat(block_expert, sub), n_active * sub, w_down, tm=tm_down, tn=512)
    return combine_rows(x, y_rows, dest.astype(jnp.int32), gates, final_gain)


def kernel(x, norm_mix, w_in, lb_logits, hgrn_norm, w_out, norm_ffn, dense_w_gate, dense_w_up,
           dense_w_down, moe_router, moe_w_gate, moe_w_up, moe_w_down, final_norm):
    B, S, D = x.shape
    depth = w_in.shape[0]
    lb_sm = jax.nn.softmax(lb_logits.astype(F32), axis=0)
    lower_bounds = jnp.cumsum(lb_sm, axis=0) - lb_sm[0:1]
    xt = x.reshape(B * S, D)
    for layer in range(depth):
        proj = norm_matmul(xt, norm_mix[layer], w_in, layer)
        o_hgrn = hgrn2(proj, lower_bounds[layer], hgrn_norm[layer], batch=B, seq=S)
        o_attn = dilated_attention(proj, batch=B, seq=S, col0=4 * HGRN_WIDTH)
        i = layer // 2
        last = layer == depth - 1
        if layer % 2 == 0:
            xt, h = out_proj_norm(o_hgrn, o_attn, xt, w_out[layer].astype(BF16), norm_ffn[layer])
            xt = dense_ffn(h, xt, dense_w_gate, dense_w_up, dense_w_down, i)
            if last:
                xt = rms_rows(xt, final_norm)
        else:
            xt, h, gates, idx = out_proj_norm(o_hgrn, o_attn, xt, w_out[layer].astype(BF16),
                                              norm_ffn[layer], moe_router[i])
            xt = moe_layer(h, xt, gates, idx, moe_w_gate, moe_w_up, moe_w_down, i,
                           final_norm if last else None)
    return xt.reshape(B, S, D)
```
